```python
import math
import jax, jax.numpy as jnp
from jax import lax
import numpy as np

D_MODEL = 1024
BATCH = 8
SEQ = 2048
DEPTH = 2
DEC_BATCH = 16
DEC_SEQ = 64
PAST_LEN = 4096

CHUNK = 64
MEM_LEN = 256
BRANCH_WIDTH = 512
N_BRANCH = 3
HG_HEADS = 4
HG_DK = 128
HG_DV = 128
HG_WIDTH = HG_HEADS * HG_DV
HG_BLOCK = 32
LB_FLOOR = 1e-30
DSA_HEADS = 8
DSA_KV_HEADS = 2
DSA_GROUP = DSA_HEADS // DSA_KV_HEADS
DSA_HEAD_DIM = 64
DSA_WIDTH = DSA_HEADS * DSA_HEAD_DIM
IDX_HEADS = 8
IDX_DIM = 32
TOPK_MAX = 256
Q_BLOCK = 64
MEM_HEADS = 4
MEM_HEAD_DIM = 128
MEM_WIDTH = MEM_HEADS * MEM_HEAD_DIM
F_DENSE = 2816
N_EXPERTS = 8
TOP_K = 2
F_EXPERT = 3584
N_DENSE = (DEPTH + 1) // 2
N_MOE = DEPTH // 2
ROPE_THETA = 10000.0
LN_EPS = 1e-5
RMS_EPS = 1e-6
NEG_INF = -1e30
ALPHA = (2 * DEPTH) ** 0.25
BETA = (8 * DEPTH) ** -0.25
IN_SIZES = (HG_HEADS * HG_DK, HG_HEADS * HG_DK, HG_HEADS * HG_DV, HG_WIDTH,
            DSA_WIDTH, DSA_KV_HEADS * DSA_HEAD_DIM, DSA_KV_HEADS * DSA_HEAD_DIM,
            IDX_HEADS * IDX_DIM, IDX_DIM, IDX_HEADS,
            MEM_WIDTH, N_BRANCH * D_MODEL)
IN_COLS = sum(IN_SIZES)

kernel_name = "hgrn2_dsa_memory_streaming_encoder_step"


def split_cols(h):
    parts, start = [], 0
    for n in IN_SIZES:
        parts.append(h[..., start:start + n])
        start += n
    return parts


def layer_norm(x, g, b):
    xf = x.astype(jnp.float32)
    mu = jnp.mean(xf, axis=-1, keepdims=True)
    var = jnp.mean(jnp.square(xf - mu), axis=-1, keepdims=True)
    y = (xf - mu) * lax.rsqrt(var + LN_EPS)
    return (y * g.astype(jnp.float32) + b.astype(jnp.float32)).astype(x.dtype)


def rope(x, pos):
    half = x.shape[-1] // 2
    inv = ROPE_THETA ** (-jnp.arange(half, dtype=jnp.float32) / half)
    ang = pos.astype(jnp.float32)[:, None] * inv[None, :]
    cos = jnp.cos(ang)[:, None, :]
    sin = jnp.sin(ang)[:, None, :]
    xf = x.astype(jnp.float32)
    x1, x2 = xf[..., :half], xf[..., half:]
    return jnp.concatenate([x1 * cos - x2 * sin, x2 * cos + x1 * sin], axis=-1).astype(x.dtype)


def hgrn2_chunkwise(q, log_f, k, v, s0):
    B, L, H, _ = q.shape
    DV = v.shape[-1]
    pad = (-L) % HG_BLOCK
    q, log_f, k, v = (jnp.pad(a.astype(jnp.float32), ((0, 0), (0, pad), (0, 0), (0, 0)))
                      for a in (q, log_f, k, v))
    n = (L + pad) // HG_BLOCK

    def to_blocks(a):
        return a.reshape(B, n, HG_BLOCK, H, a.shape[-1]).transpose(1, 0, 3, 2, 4)

    causal = jnp.tril(jnp.ones((HG_BLOCK, HG_BLOCK), dtype=bool))[:, :, None]

    def step(S, blk):
        qb, lfb, kb, vb = blk
        b = jnp.cumsum(lfb, axis=2)
        diff = b[:, :, :, None, :] - b[:, :, None, :, :]
        decay = jnp.exp(jnp.where(causal, diff, NEG_INF))
        A = jnp.einsum('bhtd,bhsd,bhtsd->bhts', qb, kb, decay)
        o = (jnp.einsum('bhts,bhsv->bhtv', A, vb)
             + jnp.einsum('bhtd,bhdv->bhtv', qb * jnp.exp(b), S))
        b_last = b[:, :, -1:]
        S = (jnp.exp(b_last)[:, :, 0, :, None] * S
             + jnp.einsum('bhsd,bhsv->bhdv', kb * jnp.exp(b_last - b), vb))
        return S, o

    S, o = lax.scan(step, s0.astype(jnp.float32),
                    (to_blocks(q), to_blocks(log_f), to_blocks(k), to_blocks(v)))
    o = o.transpose(1, 0, 3, 2, 4).reshape(B, n * HG_BLOCK, H, DV)[:, :L]
    return o, S


def dsa_block(q, qi, w, q_pos, k_all, v_all, ki_all, k_pos, topk):
    B, T = q.shape[:2]
    rel = jax.nn.relu(jnp.einsum('bthd,bsd->bths', qi, ki_all))
    score = jnp.einsum('bth,bths->bts', w, rel).astype(jnp.float32)
    admissible = (k_pos[None, :] // CHUNK) <= (q_pos[:, None] // CHUNK)
    score = jnp.where(admissible[None], score, NEG_INF)
    top_val, top_idx = lax.top_k(score, topk)
    valid = top_val > 0.5 * NEG_INF
    kg = jax.vmap(lambda kb, ib: kb[ib])(k_all, top_idx)
    vg = jax.vmap(lambda vb, ib: vb[ib])(v_all, top_idx)
    qg = q.reshape(B, T, DSA_KV_HEADS, DSA_GROUP, DSA_HEAD_DIM)
    logits = jnp.einsum('btnrd,btknd->btnrk', qg, kg).astype(jnp.float32) * (DSA_HEAD_DIM ** -0.5)
    logits = jnp.where(valid[:, :, None, None, :], logits, NEG_INF)
    p = jax.nn.softmax(logits, axis=-1).astype(v_all.dtype)
    o = jnp.einsum('btnrk,btknd->btnrd', p, vg)
    return o.reshape(B, T, DSA_WIDTH)


def mem_attend(q, mk, mv):
    logits = jnp.einsum('bthd,bmhd->bhtm', q, mk).astype(jnp.float32) * (MEM_HEAD_DIM ** -0.5)
    p = jax.nn.softmax(logits, axis=-1).astype(mv.dtype)
    return jnp.einsum('bhtm,bmhd->bthd', p, mv)


def mixer_sublayer(x, pos, s0, k_past, v_past, ki_past, mem_k, mem_v,
                   w_in, lb, hg_norm_g, w_branch, w_out):
    B, T, _ = x.shape
    h = x @ w_in
    (hq, hf, hi, hg, dq, dk, dv, iq, ik, iw, mq, gates) = split_cols(h)

    qh = jax.nn.silu(hq).reshape(B, T, HG_HEADS, HG_DK)
    z = hf.astype(jnp.float32).reshape(B, T, HG_HEADS, HG_DK)
    lbh = lb.reshape(HG_HEADS, HG_DK)
    log_f = jnp.logaddexp(jnp.log(jnp.maximum(lbh, LB_FLOOR)),
                          jnp.log1p(-lbh) + jax.nn.log_sigmoid(z))
    kf = (1.0 - lbh) * jax.nn.sigmoid(-z)
    iv = hi.reshape(B, T, HG_HEADS, HG_DV)
    o_hg, s_new = hgrn2_chunkwise(qh, log_f, kf, iv, s0)
    o_hg = o_hg * lax.rsqrt(jnp.mean(jnp.square(o_hg), axis=-1, keepdims=True) + RMS_EPS)
    o_hg = (o_hg.reshape(B, T, HG_WIDTH) * hg_norm_g.astype(jnp.float32)
            * jax.nn.sigmoid(hg.astype(jnp.float32))).astype(x.dtype)

    q = rope(dq.reshape(B, T, DSA_HEADS, DSA_HEAD_DIM), pos)
    k_new = rope(dk.reshape(B, T, DSA_KV_HEADS, DSA_HEAD_DIM), pos)
    v_new = dv.reshape(B, T, DSA_KV_HEADS, DSA_HEAD_DIM)
    qi = rope(iq.reshape(B, T, IDX_HEADS, IDX_DIM), pos)
    ki_new = rope(ik[:, :, None, :], pos)[:, :, 0]
    wts = iw * (IDX_HEADS ** -0.5)
    k_all = jnp.concatenate([k_past, k_new], axis=1)
    v_all = jnp.concatenate([v_past, v_new], axis=1)
    ki_all = jnp.concatenate([ki_past, ki_new], axis=1)
    L = k_all.shape[1]
    k_pos = jnp.arange(L, dtype=jnp.int32)
    topk = min(TOPK_MAX, L // 4)
    qblk = min(Q_BLOCK, T)
    nb = T // qblk

    def blocks(a):
        return a.reshape(B, nb, qblk, *a.shape[2:]).swapaxes(0, 1)

    o_dsa = lax.map(lambda args: dsa_block(*args, k_all, v_all, ki_all, k_pos, topk),
                    (blocks(q), blocks(qi), blocks(wts), pos.reshape(nb, qblk)))
    o_dsa = o_dsa.swapaxes(0, 1).reshape(B, T, DSA_WIDTH)

    o_mem = mem_attend(mq.reshape(B, T, MEM_HEADS, MEM_HEAD_DIM), mem_k, mem_v).reshape(B, T, MEM_WIDTH)

    br = jnp.stack([o_hg, o_dsa, o_mem], axis=2)
    proj = jnp.einsum('btnc,ncd->btnd', br, w_branch)
    g = jax.nn.sigmoid(gates.astype(jnp.float32)).reshape(B, T, N_BRANCH, D_MODEL).astype(x.dtype)
    merged = jnp.einsum('btnd,btnd->btd', g, proj)
    return merged @ w_out, s_new.astype(x.dtype), k_new, v_new, ki_new


def swiglu(x, w_gu, w_down):
    f = w_down.shape[0]
    gu = x @ w_gu
    return (jax.nn.silu(gu[..., :f]) * gu[..., f:]) @ w_down


def moe_ffn(x, w_router, w_gu, w_down):
    logits = (x @ w_router).astype(jnp.float32)
    top_v, top_i = lax.top_k(logits, TOP_K)
    gw = jax.nn.softmax(top_v, axis=-1)
    dense_gate = jnp.sum(jax.nn.one_hot(top_i, N_EXPERTS, dtype=jnp.float32) * gw[..., None], axis=-2)
    dense_gate = dense_gate.astype(x.dtype)
    out = jnp.zeros_like(x)
    for e in range(N_EXPERTS):
        out = out + dense_gate[..., e:e + 1] * swiglu(x, w_gu[e], w_down[e])
    return out


def trunk(x, pos, hg_state, k_past, v_past, ki_past, mem_k, mem_v,
          w_in, lbs, hgrn_norm_g, w_branch, w_out, ln_g, ln_b,
          ffn_w_gate_up, ffn_w_down, moe_router, moe_w_gate_up, moe_w_down):
    new_s, new_k, new_v, new_ki = [], [], [], []
    for l in range(DEPTH):
        m, s, k, v, ki = mixer_sublayer(x, pos, hg_state[l], k_past[l], v_past[l], ki_past[l],
                                        mem_k[l], mem_v[l], w_in[l], lbs[l], hgrn_norm_g[l],
                                        w_branch[l], w_out[l])
        x = layer_norm(ALPHA * x + m, ln_g[l, 0], ln_b[l, 0])
        if l % 2 == 0:
            f = swiglu(x, ffn_w_gate_up[l // 2], ffn_w_down[l // 2])
        else:
            f = moe_ffn(x, moe_router[l // 2], moe_w_gate_up[l // 2], moe_w_down[l // 2])
        x = layer_norm(ALPHA * x + f, ln_g[l, 1], ln_b[l, 1])
        new_s.append(s)
        new_k.append(k)
        new_v.append(v)
        new_ki.append(ki)
    return x, jnp.stack(new_s), jnp.stack(new_k), jnp.stack(new_v), jnp.stack(new_ki)


def setup_inputs(seed: int = 0) -> dict:
    key = jax.random.key(seed)
    ks = jax.random.split(key, 24)
    f32 = jnp.float32

    def nrm(k, shape, scale):
        return jax.random.normal(k, shape, f32) * scale

    return {
        "x_prompt": nrm(ks[0], (BATCH, SEQ, D_MODEL), 1.0),
        "x_sample": nrm(ks[1], (DEC_BATCH, DEC_SEQ, D_MODEL), 1.0),
        "cache_dsa_k": nrm(ks[2], (DEPTH, DEC_BATCH, PAST_LEN, DSA_KV_HEADS, DSA_HEAD_DIM), 1.0),
        "cache_dsa_v": nrm(ks[3], (DEPTH, DEC_BATCH, PAST_LEN, DSA_KV_HEADS, DSA_HEAD_DIM), 1.0),
        "cache_idx_k": nrm(ks[4], (DEPTH, DEC_BATCH, PAST_LEN, IDX_DIM), 1.0),
        "state_hgrn": nrm(ks[5], (DEPTH, DEC_BATCH, HG_HEADS, HG_DK, HG_DV), 0.3),
        "cache_mem_k": nrm(ks[6], (DEPTH, DEC_BATCH, MEM_LEN, MEM_HEADS, MEM_HEAD_DIM), 1.0),
        "cache_mem_v": nrm(ks[7], (DEPTH, DEC_BATCH, MEM_LEN, MEM_HEADS, MEM_HEAD_DIM), 1.0),
        "mem_prompt": nrm(ks[8], (BATCH, MEM_LEN, D_MODEL), 1.0),
        "w_in": nrm(ks[9], (DEPTH, D_MODEL, IN_COLS), D_MODEL ** -0.5),
        "hgrn_lb_logits": nrm(ks[10], (DEPTH, HG_HEADS * HG_DK), 0.5),
        "hgrn_norm_g": 1.0 + nrm(ks[11], (DEPTH, HG_WIDTH), 0.02),
        "w_branch": nrm(ks[12], (DEPTH, N_BRANCH, BRANCH_WIDTH, D_MODEL), BRANCH_WIDTH ** -0.5),
        "w_out": nrm(ks[13], (DEPTH, D_MODEL, D_MODEL), D_MODEL ** -0.5 * BETA),
        "w_mem_kv": nrm(ks[14], (DEPTH, D_MODEL, 2 * MEM_WIDTH), D_MODEL ** -0.5),
        "ln_g": 1.0 + nrm(ks[15], (DEPTH, 2, D_MODEL), 0.02),
        "ln_b": nrm(ks[16], (DEPTH, 2, D_MODEL), 0.02),
        "ffn_w_gate_up": nrm(ks[17], (N_DENSE, D_MODEL, 2 * F_DENSE), D_MODEL ** -0.5),
        "ffn_w_down": nrm(ks[18], (N_DENSE, F_DENSE, D_MODEL), F_DENSE ** -0.5 * BETA),
        "moe_router": nrm(ks[19], (N_MOE, D_MODEL, N_EXPERTS), D_MODEL ** -0.5),
        "moe_w_gate_up": nrm(ks[20], (N_MOE, N_EXPERTS, D_MODEL, 2 * F_EXPERT), D_MODEL ** -0.5),
        "moe_w_down": nrm(ks[21], (N_MOE, N_EXPERTS, F_EXPERT, D_MODEL), F_EXPERT ** -0.5 * BETA),
    }


def reference(x_prompt, x_sample, cache_dsa_k, cache_dsa_v, cache_idx_k, state_hgrn,
              cache_mem_k, cache_mem_v, mem_prompt, w_in, hgrn_lb_logits, hgrn_norm_g,
              w_branch, w_out, w_mem_kv, ln_g, ln_b, ffn_w_gate_up, ffn_w_down,
              moe_router, moe_w_gate_up, moe_w_down):
    sm = jax.nn.softmax(hgrn_lb_logits.astype(jnp.float32), axis=0)
    lbs = jnp.cumsum(sm, axis=0) - sm[0:1]
    weights = (w_in, lbs, hgrn_norm_g, w_branch, w_out, ln_g, ln_b,
               ffn_w_gate_up, ffn_w_down, moe_router, moe_w_gate_up, moe_w_down)

    B, T = x_prompt.shape[:2]
    M = mem_prompt.shape[1]
    mem_kv = jnp.einsum('bmd,ldc->lbmc', mem_prompt, w_mem_kv)
    p_mem_k = mem_kv[..., :MEM_WIDTH].reshape(DEPTH, B, M, MEM_HEADS, MEM_HEAD_DIM)
    p_mem_v = mem_kv[..., MEM_WIDTH:].reshape(DEPTH, B, M, MEM_HEADS, MEM_HEAD_DIM)
    pos_p = jnp.arange(T, dtype=jnp.int32)
    hg0 = jnp.zeros((DEPTH, B, HG_HEADS, HG_DK, HG_DV), x_prompt.dtype)
    k0 = jnp.zeros((DEPTH, B, 0, DSA_KV_HEADS, DSA_HEAD_DIM), x_prompt.dtype)
    ki0 = jnp.zeros((DEPTH, B, 0, IDX_DIM), x_prompt.dtype)
    y_prompt, p_hgrn, p_dsa_k, p_dsa_v, p_idx_k = trunk(
        x_prompt, pos_p, hg0, k0, k0, ki0, p_mem_k, p_mem_v, *weights)

    Ts = x_sample.shape[1]
    past = cache_dsa_k.shape[2]
    pos_s = past + jnp.arange(Ts, dtype=jnp.int32)
    y_sample, s_hgrn, s_dsa_k, s_dsa_v, s_idx_k = trunk(
        x_sample, pos_s, state_hgrn, cache_dsa_k, cache_dsa_v, cache_idx_k,
        cache_mem_k, cache_mem_v, *weights)

    return (y_prompt, y_sample, p_dsa_k, p_dsa_v, p_idx_k, p_hgrn, p_mem_k, p_mem_v,
            s_dsa_k, s_dsa_v, s_idx_k, s_hgrn)
```

```python
import functools
import math

import jax
import jax.numpy as jnp
from jax import lax
from jax.experimental import pallas as pl
from jax.experimental.pallas import tpu as pltpu

F32, BF16, I32 = jnp.float32, jnp.bfloat16, jnp.int32

D_MODEL = 1024
CHUNK = 64
HG_HEADS, HG_DK, HG_DV, HG_BLOCK = 4, 128, 128, 32
HG_WIDTH = HG_HEADS * HG_DV
LB_FLOOR = 1e-30
DSA_HEADS, DSA_KV_HEADS, DSA_HEAD_DIM = 8, 2, 64
DSA_GROUP = DSA_HEADS // DSA_KV_HEADS
DSA_WIDTH = DSA_HEADS * DSA_HEAD_DIM
IDX_HEADS, IDX_DIM = 8, 32
TOPK_MAX = 256
Q_BLOCK = 64
MEM_HEADS, MEM_HEAD_DIM = 4, 128
MEM_WIDTH = MEM_HEADS * MEM_HEAD_DIM
N_BRANCH = 3
N_EXPERTS, TOP_K = 8, 2
ROPE_THETA = 10000.0
LN_EPS = 1e-5
RMS_EPS = 1e-6
NEG_INF = -1e30
IN_SIZES = (HG_HEADS * HG_DK, HG_HEADS * HG_DK, HG_HEADS * HG_DV, HG_WIDTH,
            DSA_WIDTH, DSA_KV_HEADS * DSA_HEAD_DIM, DSA_KV_HEADS * DSA_HEAD_DIM,
            IDX_HEADS * IDX_DIM, IDX_DIM, IDX_HEADS, MEM_WIDTH, N_BRANCH * D_MODEL)

LANE = 128
SUBLANE = 8
VMEM_LIMIT = 48 * 1024 * 1024

C_HQ, C_HF, C_HI, C_HG = 0, 512, 1024, 1536
C_GATE = 2048
C_DQ = 5120
C_MQ = 5632
C_IQ = 6144
C_DK = 6400
C_DV = 6528
C_IK = 6656
H_COLS = 6912
IN_TN = 768

NT_DIMS = (((1,), (1,)), ((), ()))


def _cparams(sem):
    return pltpu.CompilerParams(dimension_semantics=sem, vmem_limit_bytes=VMEM_LIMIT)


def _tile(n, pref, mult=SUBLANE):
    t = min(n, pref)
    while n % t or t % mult:
        t -= 1
    return t


def _sigmoid(x):
    return 1.0 / (1.0 + jnp.exp(-x))


def _layer_norm(v, g, b):
    mu = jnp.mean(v, axis=-1, keepdims=True)
    d = v - mu
    var = jnp.mean(d * d, axis=-1, keepdims=True)
    return d * lax.rsqrt(var + LN_EPS) * g + b


def _mm_kernel(x_ref, w_ref, o_ref):
    o_ref[...] = jnp.dot(x_ref[...].astype(BF16), w_ref[...], preferred_element_type=F32)


def _matmul(x, w, tm, tn, name):
    n, k = x.shape
    nc = w.shape[1]
    return pl.pallas_call(
        _mm_kernel,
        grid=(n // tm, nc // tn),
        in_specs=[pl.BlockSpec((tm, k), lambda i, j: (i, 0)),
                  pl.BlockSpec((k, tn), lambda i, j: (0, j))],
        out_specs=pl.BlockSpec((tm, tn), lambda i, j: (i, j)),
        out_shape=jax.ShapeDtypeStruct((n, nc), F32),
        compiler_params=_cparams(("parallel", "arbitrary")),
        name=name,
    )(x, w)


def _rot_half(x, half):
    lane = lax.broadcasted_iota(I32, x.shape, 1)
    lo = (lane % (2 * half)) < half
    return jnp.where(lo, pltpu.roll(x, LANE - half, 1), pltpu.roll(x, half, 1))


def _rope_kernel(dq_ref, dk_ref, iq_ref, ik_ref, c64_ref, s64_ref, c32_ref, s32_ref, cik_ref, sik_ref,
                 q_ref, k_ref, qi_ref, ki_ref):
    c64, s64 = c64_ref[...], s64_ref[...]
    c32, s32 = c32_ref[...], s32_ref[...]
    for j in range(DSA_WIDTH // LANE):
        x = dq_ref[:, j * LANE:(j + 1) * LANE]
        q_ref[:, j * LANE:(j + 1) * LANE] = x * c64 + _rot_half(x, DSA_HEAD_DIM // 2) * s64
    x = dk_ref[...]
    k_ref[...] = x * c64 + _rot_half(x, DSA_HEAD_DIM // 2) * s64
    for j in range(IDX_HEADS * IDX_DIM // LANE):
        x = iq_ref[:, j * LANE:(j + 1) * LANE]
        qi_ref[:, j * LANE:(j + 1) * LANE] = x * c32 + _rot_half(x, IDX_DIM // 2) * s32
    x = ik_ref[...]
    ki_ref[...] = x * cik_ref[...] + _rot_half(x, IDX_DIM // 2) * sik_ref[...]


def _rope_tables(pos):
    lane = jnp.arange(LANE)
    p = pos.astype(F32)[:, None]
    h64 = DSA_HEAD_DIM // 2
    inv64 = ROPE_THETA ** (-jnp.arange(h64, dtype=F32) / h64)
    a64 = p * inv64[lane % h64][None, :]
    c64 = jnp.cos(a64)
    s64 = jnp.sin(a64) * jnp.where((lane % DSA_HEAD_DIM) < h64, -1.0, 1.0)[None, :]
    h32 = IDX_DIM // 2
    inv32 = ROPE_THETA ** (-jnp.arange(h32, dtype=F32) / h32)
    a32 = p * inv32[lane % h32][None, :]
    c32 = jnp.cos(a32)
    s32 = jnp.sin(a32) * jnp.where((lane % IDX_DIM) < h32, -1.0, 1.0)[None, :]
    live = (lane < IDX_DIM)[None, :]
    cik = jnp.where(live, c32, 1.0)
    sik = jnp.where(live, s32, 0.0)
    return tuple(t.astype(F32) for t in (c64, s64, c32, s32, cik, sik))


def _rope(h, tables, row0, nb, t):
    tr = _tile(t, 256)
    nt = t // tr
    rb0 = row0 // tr
    rows = nb * t

    def hspec(width, col):
        return pl.BlockSpec((tr, width), lambda i: (rb0 + i, col // width))

    tspec = pl.BlockSpec((tr, LANE), lambda i: (i % nt, 0))

    def ospec(width):
        return pl.BlockSpec((tr, width), lambda i: (i, 0))

    return pl.pallas_call(
        _rope_kernel,
        grid=(rows // tr,),
        in_specs=[hspec(DSA_WIDTH, C_DQ), hspec(LANE, C_DK), hspec(IDX_HEADS * IDX_DIM, C_IQ), hspec(LANE, C_IK)]
        + [tspec] * 6,
        out_specs=[ospec(DSA_WIDTH), ospec(LANE), ospec(IDX_HEADS * IDX_DIM), ospec(LANE)],
        out_shape=[jax.ShapeDtypeStruct((rows, DSA_WIDTH), F32), jax.ShapeDtypeStruct((rows, LANE), F32),
                   jax.ShapeDtypeStruct((rows, IDX_HEADS * IDX_DIM), F32), jax.ShapeDtypeStruct((rows, LANE), F32)],
        compiler_params=_cparams(("parallel",)),
        name="rope",
    )(h, h, h, h, *tables)


def _hgrn_kernel(q_ref, f_ref, i_ref, g_ref, lb_ref, ng_ref, s0_ref, o_ref, s_ref,
                 st_s, b_s, k_s, v_s, q_s, o_s, *, tt):
    t = pl.program_id(2)
    c = HG_BLOCK

    @pl.when(t == 0)
    def _():
        st_s[...] = s0_ref[...].T

    lb = lb_ref[...]
    z = f_ref[...]
    lf = jnp.log(jnp.maximum(lb, LB_FLOOR) + (1.0 - lb) * _sigmoid(z))
    k_s[...] = (1.0 - lb) * _sigmoid(-z)
    hq = q_ref[...]
    q_s[...] = hq * _sigmoid(hq)
    v_s[...] = i_ref[...]

    r = lax.broadcasted_iota(I32, (tt, tt), 0)
    cc = lax.broadcasted_iota(I32, (tt, tt), 1)
    tri = jnp.where(((r // c) == (cc // c)) & (cc <= r), 1.0, 0.0).astype(BF16)
    hi = lf.astype(BF16)
    r1 = lf - hi.astype(F32)
    mid = r1.astype(BF16)
    lo = (r1 - mid.astype(F32)).astype(BF16)
    b_s[...] = (jnp.dot(tri, hi, preferred_element_type=F32) + jnp.dot(tri, mid, preferred_element_type=F32)
                + jnp.dot(tri, lo, preferred_element_type=F32))

    row8 = lax.broadcasted_iota(I32, (SUBLANE, LANE), 0)
    nsub = c // SUBLANE

    def block(j, carry):
        base = pl.multiple_of(j * c, c)
        bj = b_s[pl.ds(base, c), :]
        qj = q_s[pl.ds(base, c), :]
        kj = k_s[pl.ds(base, c), :]
        vj = v_s[pl.ds(base, c), :]
        blast = b_s[pl.ds(base + c - 1, 1), :]
        oc = [jnp.zeros((SUBLANE, LANE), F32) for _ in range(nsub)]
        for s in range(c):
            bs = b_s[pl.ds(base + s, 1), :]
            ks = k_s[pl.ds(base + s, 1), :]
            vs = v_s[pl.ds(base + s, 1), :]
            for u in range(s // SUBLANE, nsub):
                d = bj[u * SUBLANE:(u + 1) * SUBLANE] - bs
                if u == s // SUBLANE:
                    d = jnp.where(row8 >= (s % SUBLANE), d, NEG_INF)
                a = jnp.sum(qj[u * SUBLANE:(u + 1) * SUBLANE] * jnp.exp(d) * ks, axis=-1, keepdims=True)
                oc[u] = oc[u] + a * vs
        o_intra = jnp.concatenate(oc, axis=0)
        st = st_s[...]
        qe = qj * jnp.exp(bj)
        o_inter = lax.dot_general(qe.astype(BF16), st.astype(BF16), NT_DIMS, preferred_element_type=F32)
        ke = kj * jnp.exp(blast - bj)
        upd = jnp.dot(vj.T.astype(BF16), ke.astype(BF16), preferred_element_type=F32)
        st_s[...] = jnp.exp(blast) * st + upd
        o_s[pl.ds(base, c), :] = o_intra + o_inter
        return carry

    lax.fori_loop(0, tt // c, block, 0)

    o = o_s[...]
    o = o * lax.rsqrt(jnp.mean(o * o, axis=-1, keepdims=True) + RMS_EPS)
    o_ref[...] = o * ng_ref[...] * _sigmoid(g_ref[...])

    @pl.when(t == pl.num_programs(2) - 1)
    def _():
        s_ref[...] = st_s[...].T


def _hgrn(h, lb, ng, s0, row0, nb, t):
    assert t % HG_BLOCK == 0
    tt = _tile(t, 256, HG_BLOCK)
    nt = t // tt
    rb0 = row0 // tt

    def hspec(col):
        return pl.BlockSpec((tt, LANE), lambda b, hd, i: (rb0 + b * nt + i, col // LANE + hd))

    vspec = pl.BlockSpec((1, LANE), lambda b, hd, i: (0, hd))
    sspec = pl.BlockSpec((None, None, HG_DK, HG_DV), lambda b, hd, i: (b, hd, 0, 0))
    return pl.pallas_call(
        functools.partial(_hgrn_kernel, tt=tt),
        grid=(nb, HG_HEADS, nt),
        in_specs=[hspec(C_HQ), hspec(C_HF), hspec(C_HI), hspec(C_HG), vspec, vspec, sspec],
        out_specs=[pl.BlockSpec((tt, LANE), lambda b, hd, i: (b * nt + i, hd)), sspec],
        out_shape=[jax.ShapeDtypeStruct((nb * t, HG_WIDTH), F32),
                   jax.ShapeDtypeStruct((nb, HG_HEADS, HG_DK, HG_DV), F32)],
        scratch_shapes=[pltpu.VMEM((HG_DV, HG_DK), F32)] + [pltpu.VMEM((tt, LANE), F32)] * 5,
        compiler_params=_cparams(("parallel", "parallel", "arbitrary")),
        name="hgrn2",
    )(h, h, h, h, lb, ng, s0)


def _dsa_kernel(q_ref, qi_ref, w_ref, k_ref, v_ref, ki_ref, o_ref, kb_s, vb_s, kib_s, sc_s,
                *, past, topk, kc, jbits):
    qb = pl.program_id(1)
    kf = float(topk)

    @pl.when(qb == 0)
    def _():
        kb_s[...] = k_ref[...].astype(BF16)
        vb_s[...] = v_ref[...].astype(BF16)
        kib_s[...] = ki_ref[...].astype(BF16)

    n_adm = past + (qb + 1) * Q_BLOCK
    nkc = (n_adm + kc - 1) // kc
    qi = qi_ref[...]
    wv = w_ref[...] * (IDX_HEADS ** -0.5)
    qih = [qi[:, h * IDX_DIM:(h + 1) * IDX_DIM].astype(BF16) for h in range(IDX_HEADS)]
    wh = [wv[:, IDX_DIM + h:IDX_DIM + h + 1] for h in range(IDX_HEADS)]
    lane = lax.broadcasted_iota(I32, (Q_BLOCK, kc), 1)

    def score_chunk(c, carry):
        mn, mx = carry
        base = pl.multiple_of(c * kc, kc)
        kic = kib_s[pl.ds(base, kc), :]
        sc = jnp.zeros((Q_BLOCK, kc), F32)
        for h in range(IDX_HEADS):
            rel = lax.dot_general(qih[h], kic, NT_DIMS, preferred_element_type=F32)
            sc = sc + wh[h] * jnp.maximum(rel, 0.0)
        adm = (base + lane) < n_adm
        sc_s[c] = jnp.where(adm, sc, NEG_INF)
        mn = jnp.minimum(mn, jnp.min(jnp.where(adm, sc, jnp.inf), axis=-1, keepdims=True))
        mx = jnp.maximum(mx, jnp.max(jnp.where(adm, sc, -jnp.inf), axis=-1, keepdims=True))
        return mn, mx

    mn, mx = lax.fori_loop(0, nkc, score_chunk,
                           (jnp.full((Q_BLOCK, 1), jnp.inf, F32), jnp.full((Q_BLOCK, 1), -jnp.inf, F32)))

    def count(pred):
        def body(c, acc):
            return acc + jnp.where(pred(c, sc_s[c]), 1.0, 0.0)
        acc = lax.fori_loop(0, nkc, body, jnp.zeros((Q_BLOCK, kc), F32))
        return jnp.sum(acc, axis=-1, keepdims=True)

    n_adm_f = jnp.asarray(n_adm, F32)
    cl0 = jnp.full((Q_BLOCK, 1), jnp.where(n_adm <= topk, kf, n_adm_f), F32)
    hi0 = mx + (jnp.abs(mx) + 1.0)

    def mid_of(lo, hi):
        return lo + (hi - lo) * 0.5

    def bis_cond(cy):
        lo, hi, cl, ch, it = cy
        mid = mid_of(lo, hi)
        active = (cl != kf) & (mid > lo) & (mid < hi)
        return jnp.logical_and(it < 2200, jnp.max(jnp.where(active, 1.0, 0.0)) > 0.0)

    def bis_body(cy):
        lo, hi, cl, ch, it = cy
        mid = mid_of(lo, hi)
        cnt = count(lambda c, s: s >= mid)
        inside = (mid > lo) & (mid < hi)
        up = inside & (cnt >= kf)
        dn = inside & (cnt < kf)
        return (jnp.where(up, mid, lo), jnp.where(dn, mid, hi), jnp.where(up, cnt, cl), jnp.where(dn, cnt, ch), it + 1)

    lo, hi, cl, ch, _ = lax.while_loop(bis_cond, bis_body, (mn, hi0, cl0, jnp.zeros((Q_BLOCK, 1), F32), jnp.int32(0)))

    need = kf - ch

    def tie_limit():
        def bit(i, j):
            jt = j + lax.shift_left(jnp.int32(1), jnp.asarray(jbits - 1 - i, I32))
            cnt = count(lambda c, s: (s >= lo) & (s < hi) & ((c * kc + lane) < jt))
            return jnp.where(cnt <= need, jt, j)
        return lax.fori_loop(0, jbits, bit, jnp.zeros((Q_BLOCK, 1), I32))

    has_tie = jnp.max(jnp.where(cl > kf, 1.0, 0.0)) > 0.0
    jl = lax.cond(has_tie, tie_limit, lambda: jnp.full((Q_BLOCK, 1), 2 ** 30, I32))

    q = q_ref[...]
    qg = [jnp.concatenate([q[:, (n * DSA_GROUP + r) * DSA_HEAD_DIM:(n * DSA_GROUP + r + 1) * DSA_HEAD_DIM]
                           for r in range(DSA_GROUP)], axis=0).astype(BF16) for n in range(DSA_KV_HEADS)]
    rows = DSA_GROUP * Q_BLOCK
    scale = DSA_HEAD_DIM ** -0.5

    def att_chunk(c, carry):
        base = pl.multiple_of(c * kc, kc)
        s = sc_s[c]
        sel = ((s >= hi) | ((s >= lo) & ((base + lane) < jl))) & (s > 0.5 * NEG_INF)
        self = jnp.where(sel, 1.0, 0.0)
        sel4 = jnp.concatenate([self] * DSA_GROUP, axis=0) > 0.5
        kch = kb_s[pl.ds(base, kc), :]
        vch = vb_s[pl.ds(base, kc), :]
        out = []
        for n in range(DSA_KV_HEADS):
            m, l, acc = carry[n]
            kn = kch[:, n * DSA_HEAD_DIM:(n + 1) * DSA_HEAD_DIM]
            vn = vch[:, n * DSA_HEAD_DIM:(n + 1) * DSA_HEAD_DIM]
            lg = lax.dot_general(qg[n], kn, NT_DIMS, preferred_element_type=F32) * scale
            lg = jnp.where(sel4, lg, NEG_INF)
            m_new = jnp.maximum(m, jnp.max(lg, axis=-1, keepdims=True))
            alpha = jnp.exp(m - m_new)
            p = jnp.exp(lg - m_new)
            l = alpha * l + jnp.sum(p, axis=-1, keepdims=True)
            acc = alpha * acc + jnp.dot(p.astype(BF16), vn, preferred_element_type=F32)
            out.append((m_new, l, acc))
        return tuple(out)

    init = tuple((jnp.full((rows, 1), NEG_INF, F32), jnp.zeros((rows, 1), F32), jnp.zeros((rows, DSA_HEAD_DIM), F32))
                 for _ in range(DSA_KV_HEADS))
    res = lax.fori_loop(0, nkc, att_chunk, init)
    pieces = []
    for n in range(DSA_KV_HEADS):
        _, l, acc = res[n]
        og = acc / l
        pieces += [og[r * Q_BLOCK:(r + 1) * Q_BLOCK] for r in range(DSA_GROUP)]
    o_ref[...] = jnp.concatenate(pieces, axis=1)


def _dsa(q, qi, kiw, k_all, v_all, ki_all, nb, t, past, topk):
    lp = k_all.shape[1]
    kc = 256
    assert lp % kc == 0 and kc >= topk and t % Q_BLOCK == 0
    nq = t // Q_BLOCK
    jbits = int(math.ceil(math.log2(lp))) + 1

    def qspec(width):
        return pl.BlockSpec((Q_BLOCK, width), lambda b, i: (b * nq + i, 0))

    def kspec(width):
        return pl.BlockSpec((None, lp, width), lambda b, i: (b, 0, 0))

    return pl.pallas_call(
        functools.partial(_dsa_kernel, past=past, topk=topk, kc=kc, jbits=jbits),
        grid=(nb, nq),
        in_specs=[qspec(DSA_WIDTH), qspec(IDX_HEADS * IDX_DIM), qspec(LANE), kspec(LANE), kspec(LANE), kspec(IDX_DIM)],
        out_specs=qspec(DSA_WIDTH),
        out_shape=jax.ShapeDtypeStruct((nb * t, DSA_WIDTH), F32),
        scratch_shapes=[pltpu.VMEM((lp, LANE), BF16), pltpu.VMEM((lp, LANE), BF16), pltpu.VMEM((lp, IDX_DIM), BF16),
                        pltpu.VMEM((lp // kc, Q_BLOCK, kc), F32)],
        compiler_params=_cparams(("parallel", "arbitrary")),
        name="dsa",
    )(q, qi, kiw, k_all, v_all, ki_all)


def _mem_kernel(q_ref, mk_ref, mv_ref, o_ref):
    scale = MEM_HEAD_DIM ** -0.5
    for h in range(MEM_HEADS):
        sl = slice(h * MEM_HEAD_DIM, (h + 1) * MEM_HEAD_DIM)
        qh = q_ref[:, sl].astype(BF16)
        kh = mk_ref[:, sl].astype(BF16)
        vh = mv_ref[:, sl].astype(BF16)
        lg = lax.dot_general(qh, kh, NT_DIMS, preferred_element_type=F32) * scale
        p = jnp.exp(lg - jnp.max(lg, axis=-1, keepdims=True))
        o = jnp.dot(p.astype(BF16), vh, preferred_element_type=F32)
        o_ref[:, sl] = o / jnp.sum(p, axis=-1, keepdims=True)


def _mem_attn(h, mk, mv, kcol, vcol, row0, nb, t):
    tq = _tile(t, 256)
    nt = t // tq
    rb0 = row0 // tq
    m = mk.shape[1]
    return pl.pallas_call(
        _mem_kernel,
        grid=(nb, nt),
        in_specs=[pl.BlockSpec((tq, MEM_WIDTH), lambda b, i: (rb0 + b * nt + i, C_MQ // MEM_WIDTH)),
                  pl.BlockSpec((None, m, MEM_WIDTH), lambda b, i: (b, 0, kcol)),
                  pl.BlockSpec((None, m, MEM_WIDTH), lambda b, i: (b, 0, vcol))],
        out_specs=pl.BlockSpec((tq, MEM_WIDTH), lambda b, i: (b * nt + i, 0)),
        out_shape=jax.ShapeDtypeStruct((nb * t, MEM_WIDTH), F32),
        compiler_params=_cparams(("parallel", "parallel")),
        name="mem_attn",
    )(h, mk, mv)


def _merge_kernel(bh_ref, bd_ref, bm_ref, g0_ref, g1_ref, g2_ref, x_ref, wb_ref, wo_ref, lg_ref, lb_ref, o_ref,
                  *, alpha):
    merged = None
    for br_ref, g_ref, n in ((bh_ref, g0_ref, 0), (bd_ref, g1_ref, 1), (bm_ref, g2_ref, 2)):
        proj = jnp.dot(br_ref[...].astype(BF16), wb_ref[n], preferred_element_type=F32)
        term = _sigmoid(g_ref[...]) * proj
        merged = term if merged is None else merged + term
    m = jnp.dot(merged.astype(BF16), wo_ref[...], preferred_element_type=F32)
    o_ref[...] = _layer_norm(alpha * x_ref[...] + m, lg_ref[...], lb_ref[...])


def _merge(o_hg, o_dsa, o_mem, h, x, wb, wo, lng, lnb, alpha):
    n = x.shape[0]
    tm = _tile(n, 256)
    bspec = pl.BlockSpec((tm, 512), lambda i: (i, 0))

    def gspec(j):
        return pl.BlockSpec((tm, D_MODEL), lambda i: (i, C_GATE // D_MODEL + j))

    xspec = pl.BlockSpec((tm, D_MODEL), lambda i: (i, 0))
    vspec = pl.BlockSpec((1, D_MODEL), lambda i: (0, 0))
    return pl.pallas_call(
        functools.partial(_merge_kernel, alpha=alpha),
        grid=(n // tm,),
        in_specs=[bspec, bspec, bspec, gspec(0), gspec(1), gspec(2), xspec,
                  pl.BlockSpec((N_BRANCH, 512, D_MODEL), lambda i: (0, 0, 0)),
                  pl.BlockSpec((D_MODEL, D_MODEL), lambda i: (0, 0)), vspec, vspec],
        out_specs=xspec,
        out_shape=jax.ShapeDtypeStruct((n, D_MODEL), F32),
        compiler_params=_cparams(("parallel",)),
        name="merge",
    )(o_hg, o_dsa, o_mem, h, h, h, x, wb, wo, lng, lnb)


def _swiglu_partial(xb, wg_ref, wu_ref, wd_ref):
    g = jnp.dot(xb, wg_ref[...], preferred_element_type=F32)
    u = jnp.dot(xb, wu_ref[...], preferred_element_type=F32)
    a = (g * _sigmoid(g)) * u
    return jnp.dot(a.astype(BF16), wd_ref[...], preferred_element_type=F32)


def _ffn_kernel(x_ref, wg_ref, wu_ref, wd_ref, lg_ref, lb_ref, o_ref, xb_s, acc_s, *, alpha):
    k = pl.program_id(1)

    @pl.when(k == 0)
    def _():
        xb_s[...] = x_ref[...].astype(BF16)
        acc_s[...] = jnp.zeros_like(acc_s)

    acc_s[...] += _swiglu_partial(xb_s[...], wg_ref, wu_ref, wd_ref)

    @pl.when(k == pl.num_programs(1) - 1)
    def _():
        o_ref[...] = _layer_norm(alpha * x_ref[...] + acc_s[...], lg_ref[...], lb_ref[...])


def _ffn_dense(x, w_gu, w_down, lng, lnb, alpha):
    n = x.shape[0]
    f = w_down.shape[0]
    tm = _tile(n, 512)
    fc = 256
    nk = f // fc
    xspec = pl.BlockSpec((tm, D_MODEL), lambda i, k: (i, 0))
    vspec = pl.BlockSpec((1, D_MODEL), lambda i, k: (0, 0))
    return pl.pallas_call(
        functools.partial(_ffn_kernel, alpha=alpha),
        grid=(n // tm, nk),
        in_specs=[xspec,
                  pl.BlockSpec((D_MODEL, fc), lambda i, k: (0, k)),
                  pl.BlockSpec((D_MODEL, fc), lambda i, k: (0, nk + k)),
                  pl.BlockSpec((fc, D_MODEL), lambda i, k: (k, 0)), vspec, vspec],
        out_specs=xspec,
        out_shape=jax.ShapeDtypeStruct((n, D_MODEL), F32),
        scratch_shapes=[pltpu.VMEM((tm, D_MODEL), BF16), pltpu.VMEM((tm, D_MODEL), F32)],
        compiler_params=_cparams(("parallel", "arbitrary")),
        name="ffn_dense",
    )(x, w_gu, w_gu, w_down, lng, lnb)


def _router_kernel(x_ref, w_ref, e_ref, g_ref):
    x = x_ref[...]
    w = w_ref[...]
    xh = x.astype(BF16)
    xl = (x - xh.astype(F32)).astype(BF16)
    wh = w.astype(BF16)
    wl = (w - wh.astype(F32)).astype(BF16)
    lg = (jnp.dot(xh, wh, preferred_element_type=F32) + jnp.dot(xh, wl, preferred_element_type=F32)
          + jnp.dot(xl, wh, preferred_element_type=F32))
    lane = lax.broadcasted_iota(I32, lg.shape, 1).astype(F32)
    lg = jnp.where(lane < N_EXPERTS, lg, -jnp.inf)
    m1 = jnp.max(lg, axis=-1, keepdims=True)
    i1 = jnp.min(jnp.where(lg == m1, lane, float(LANE)), axis=-1, keepdims=True)
    lg2 = jnp.where(lane == i1, -jnp.inf, lg)
    m2 = jnp.max(lg2, axis=-1, keepdims=True)
    i2 = jnp.min(jnp.where(lg2 == m2, lane, float(LANE)), axis=-1, keepdims=True)
    e = jnp.exp(m2 - m1)
    g1 = 1.0 / (1.0 + e)
    g2 = e / (1.0 + e)
    e_ref[...] = jnp.where(lane == 0.0, i1, jnp.where(lane == 1.0, i2, 0.0)).astype(I32)
    g_ref[...] = jnp.where(lane == 0.0, g1, jnp.where(lane == 1.0, g2, 0.0))


def _router(x, w_router):
    n = x.shape[0]
    tm = _tile(n, 512)
    wp = jnp.pad(w_router, ((0, 0), (0, LANE - N_EXPERTS)))
    ospec = pl.BlockSpec((tm, LANE), lambda i: (i, 0))
    return pl.pallas_call(
        _router_kernel,
        grid=(n // tm,),
        in_specs=[pl.BlockSpec((tm, D_MODEL), lambda i: (i, 0)), pl.BlockSpec((D_MODEL, LANE), lambda i: (0, 0))],
        out_specs=[ospec, ospec],
        out_shape=[jax.ShapeDtypeStruct((n, LANE), I32), jax.ShapeDtypeStruct((n, LANE), F32)],
        compiler_params=_cparams(("parallel",)),
        name="router",
    )(x, wp)


def _row_copy(src, dst, s, d, sem):
    return pltpu.make_async_copy(src.at[pl.ds(s, 1)], dst.at[pl.ds(d, 1)], sem)


def _dispatch_kernel(pos_ref, x_hbm, xs_in, xs_hbm, sem, *, tb):
    del xs_in
    i = pl.program_id(0)

    def start(r, c):
        t = i * tb + r
        for j in range(TOP_K):
            _row_copy(x_hbm, xs_hbm, t, pos_ref[TOP_K * t + j], sem).start()
        return c

    lax.fori_loop(0, tb, start, 0)

    def wait(r, c):
        for j in range(TOP_K):
            _row_copy(x_hbm, xs_hbm, 0, 0, sem).wait()
        return c

    lax.fori_loop(0, tb, wait, 0)


def _dispatch(pos, x, n_rows):
    n = x.shape[0]
    tb = _tile(n, 256)
    xs0 = jnp.zeros((n_rows, D_MODEL), F32)
    return pl.pallas_call(
        functools.partial(_dispatch_kernel, tb=tb),
        grid_spec=pltpu.PrefetchScalarGridSpec(
            num_scalar_prefetch=1, grid=(n // tb,),
            in_specs=[pl.BlockSpec(memory_space=pl.ANY), pl.BlockSpec(memory_space=pl.ANY)],
            out_specs=pl.BlockSpec(memory_space=pl.ANY),
            scratch_shapes=[pltpu.SemaphoreType.DMA(())]),
        out_shape=jax.ShapeDtypeStruct((n_rows, D_MODEL), F32),
        input_output_aliases={2: 0},
        compiler_params=pltpu.CompilerParams(dimension_semantics=("arbitrary",), has_side_effects=True),
        name="moe_dispatch",
    )(pos, x, xs0)


def _expert_kernel(te_ref, nu_ref, x_ref, wg_ref, wu_ref, wd_ref, o_ref, xb_s, acc_s):
    del te_ref
    i = pl.program_id(0)
    k = pl.program_id(1)
    used = i < nu_ref[0]

    @pl.when(jnp.logical_and(used, k == 0))
    def _():
        xb_s[...] = x_ref[...].astype(BF16)
        acc_s[...] = jnp.zeros_like(acc_s)

    @pl.when(used)
    def _():
        acc_s[...] += _swiglu_partial(xb_s[...], wg_ref, wu_ref, wd_ref)

    @pl.when(k == pl.num_programs(1) - 1)
    def _():
        o_ref[...] = jnp.where(used, acc_s[...], 0.0)


def _experts(tile_expert, n_used, xs, w_gu, w_down, tm):
    n_rows = xs.shape[0]
    f = w_down.shape[1]
    fc = 512
    nk = f // fc
    xspec = pl.BlockSpec((tm, D_MODEL), lambda i, k, te, nu: (i, 0))
    return pl.pallas_call(
        _expert_kernel,
        grid_spec=pltpu.PrefetchScalarGridSpec(
            num_scalar_prefetch=2, grid=(n_rows // tm, nk),
            in_specs=[xspec,
                      pl.BlockSpec((None, D_MODEL, fc), lambda i, k, te, nu: (te[i], 0, k)),
                      pl.BlockSpec((None, D_MODEL, fc), lambda i, k, te, nu: (te[i], 0, nk + k)),
                      pl.BlockSpec((None, fc, D_MODEL), lambda i, k, te, nu: (te[i], k, 0))],
            out_specs=xspec,
            scratch_shapes=[pltpu.VMEM((tm, D_MODEL), BF16), pltpu.VMEM((tm, D_MODEL), F32)]),
        out_shape=jax.ShapeDtypeStruct((n_rows, D_MODEL), F32),
        compiler_params=_cparams(("arbitrary", "arbitrary")),
        name="moe_experts",
    )(tile_expert, n_used, xs, w_gu, w_gu, w_down)


def _combine_kernel(pos_ref, ys_hbm, g_ref, x_ref, lg_ref, lb_ref, o_ref, buf, sem, *, tb, alpha):
    i = pl.program_id(0)

    def start(r, c):
        t = i * tb + r
        for j in range(TOP_K):
            _row_copy(ys_hbm, buf.at[j], pos_ref[TOP_K * t + j], r, sem.at[j]).start()
        return c

    lax.fori_loop(0, tb, start, 0)

    def wait(r, c):
        for j in range(TOP_K):
            _row_copy(ys_hbm, buf.at[j], 0, 0, sem.at[j]).wait()
        return c

    lax.fori_loop(0, tb, wait, 0)
    g = g_ref[...]
    f = g[:, 0:1] * buf[0] + g[:, 1:2] * buf[1]
    o_ref[...] = _layer_norm(alpha * x_ref[...] + f, lg_ref[...], lb_ref[...])


def _combine(pos, ys, gates, x, lng, lnb, alpha):
    n = x.shape[0]
    tb = _tile(n, 256)
    xspec = pl.BlockSpec((tb, D_MODEL), lambda i, p: (i, 0))
    vspec = pl.BlockSpec((1, D_MODEL), lambda i, p: (0, 0))
    return pl.pallas_call(
        functools.partial(_combine_kernel, tb=tb, alpha=alpha),
        grid_spec=pltpu.PrefetchScalarGridSpec(
            num_scalar_prefetch=1, grid=(n // tb,),
            in_specs=[pl.BlockSpec(memory_space=pl.ANY), pl.BlockSpec((tb, LANE), lambda i, p: (i, 0)),
                      xspec, vspec, vspec],
            out_specs=xspec,
            scratch_shapes=[pltpu.VMEM((TOP_K, tb, D_MODEL), F32), pltpu.SemaphoreType.DMA((TOP_K,))]),
        out_shape=jax.ShapeDtypeStruct((n, D_MODEL), F32),
        compiler_params=_cparams(("arbitrary",)),
        name="moe_combine",
    )(pos, ys, gates, x, lng, lnb)


def _moe(x, w_router, w_gu, w_down, lng, lnb, alpha):
    n = x.shape[0]
    tm = _tile(n, 512)
    eidx, gates = _router(x, w_router)
    e2 = eidx[:, :TOP_K].reshape(-1)
    onehot = (e2[:, None] == jnp.arange(N_EXPERTS, dtype=I32)[None, :]).astype(I32)
    csum = jnp.cumsum(onehot, axis=0)
    rank = jnp.take_along_axis(csum, e2[:, None], axis=1)[:, 0] - 1
    counts = csum[-1]
    padded = ((counts + tm - 1) // tm) * tm
    ends = jnp.cumsum(padded)
    pos = ((ends - padded)[e2] + rank).astype(I32)
    n_rows = TOP_K * n + N_EXPERTS * tm
    n_tiles = n_rows // tm
    tile_expert = jnp.minimum(jnp.searchsorted(ends, jnp.arange(n_tiles, dtype=I32) * tm, side="right"),
                              N_EXPERTS - 1).astype(I32)
    n_used = (ends[-1:] // tm).astype(I32)
    xs = _dispatch(pos, x, n_rows)
    ys = _experts(tile_expert, n_used, xs, w_gu, w_down, tm)
    return _combine(pos, ys, gates, x, lng, lnb, alpha)


def _pack_w_in(w):
    d = w.shape[0]
    parts, start = [], 0
    for n in IN_SIZES:
        parts.append(w[:, start:start + n])
        start += n
    hq, hf, hi, hg, dq, dk, dv, iq, ik, iw, mq, gates = parts
    ikw = jnp.concatenate([ik, iw, jnp.zeros((d, LANE - IDX_DIM - IDX_HEADS), w.dtype)], axis=1)
    tail = jnp.zeros((d, H_COLS - (C_IK + LANE)), w.dtype)
    return jnp.concatenate([hq, hf, hi, hg, gates, dq, mq, iq, dk, dv, ikw, tail], axis=1).astype(BF16)


def kernel(x_prompt, x_sample, cache_dsa_k, cache_dsa_v, cache_idx_k, state_hgrn, cache_mem_k, cache_mem_v,
           mem_prompt, w_in, hgrn_lb_logits, hgrn_norm_g, w_branch, w_out, w_mem_kv, ln_g, ln_b, ffn_w_gate_up,
           ffn_w_down, moe_router, moe_w_gate_up, moe_w_down):
    depth = w_in.shape[0]
    alpha = (2 * depth) ** 0.25
    bp, tp, d = x_prompt.shape
    bs, ts, _ = x_sample.shape
    past = cache_dsa_k.shape[2]
    m = mem_prompt.shape[1]
    np_rows, ns_rows = bp * tp, bs * ts

    sm = jax.nn.softmax(hgrn_lb_logits.astype(F32), axis=0)
    lbs = jnp.cumsum(sm, axis=0) - sm[0:1]

    x = jnp.concatenate([x_prompt.reshape(np_rows, d), x_sample.reshape(ns_rows, d)], axis=0)
    n = x.shape[0]
    tables_p = _rope_tables(jnp.arange(tp, dtype=I32))
    tables_s = _rope_tables(past + jnp.arange(ts, dtype=I32))
    mem_flat = mem_prompt.reshape(bp * m, d)
    zeros_state = jnp.zeros((bp, HG_HEADS, HG_DK, HG_DV), F32)
    kc = 256
    lp_s = ((past + ts + kc - 1) // kc) * kc

    outs = {k: [] for k in ("pk", "pv", "pki", "ph", "pmk", "pmv", "sk", "sv", "ski", "sh")}
    for l in range(depth):
        h = _matmul(x, _pack_w_in(w_in[l]), _tile(n, 512), IN_TN, "in_proj")
        mem_kv = _matmul(mem_flat, w_mem_kv[l].astype(BF16), _tile(bp * m, 512), 512, "mem_kv").reshape(bp, m, 2 * MEM_WIDTH)
        lb = lbs[l].reshape(1, HG_WIDTH)
        ng = hgrn_norm_g[l].reshape(1, HG_WIDTH).astype(F32)

        qp, kp, qip, kip = _rope(h, tables_p, 0, bp, tp)
        vp = h[:np_rows, C_DV:C_DV + LANE]
        kip32 = kip[:, :IDX_DIM]
        ohg_p, sh_p = _hgrn(h, lb, ng, zeros_state, 0, bp, tp)
        odsa_p = _dsa(qp, qip, kip, kp.reshape(bp, tp, LANE), vp.reshape(bp, tp, LANE), kip32.reshape(bp, tp, IDX_DIM),
                      bp, tp, 0, min(TOPK_MAX, tp // 4))
        omem_p = _mem_attn(h, mem_kv, mem_kv, 0, 1, 0, bp, tp)

        qs, ks, qis, kis = _rope(h, tables_s, np_rows, bs, ts)
        vs = h[np_rows:, C_DV:C_DV + LANE]
        kis32 = kis[:, :IDX_DIM]
        npad = lp_s - past - ts
        k_all = jnp.concatenate([cache_dsa_k[l].reshape(bs, past, LANE), ks.reshape(bs, ts, LANE),
                                 jnp.zeros((bs, npad, LANE), F32)], axis=1)
        v_all = jnp.concatenate([cache_dsa_v[l].reshape(bs, past, LANE), vs.reshape(bs, ts, LANE),
                                 jnp.zeros((bs, npad, LANE), F32)], axis=1)
        ki_all = jnp.concatenate([cache_idx_k[l], kis32.reshape(bs, ts, IDX_DIM),
                                  jnp.zeros((bs, npad, IDX_DIM), F32)], axis=1)
        ohg_s, sh_s = _hgrn(h, lb, ng, state_hgrn[l], np_rows, bs, ts)
        odsa_s = _dsa(qs, qis, kis, k_all, v_all, ki_all, bs, ts, past, min(TOPK_MAX, (past + ts) // 4))
        omem_s = _mem_attn(h, cache_mem_k[l].reshape(bs, m, MEM_WIDTH), cache_mem_v[l].reshape(bs, m, MEM_WIDTH),
                           0, 0, np_rows, bs, ts)

        o_hg = jnp.concatenate([ohg_p, ohg_s], axis=0)
        o_dsa = jnp.concatenate([odsa_p, odsa_s], axis=0)
        o_mem = jnp.concatenate([omem_p, omem_s], axis=0)
        x = _merge(o_hg, o_dsa, o_mem, h, x, w_branch[l].astype(BF16), w_out[l].astype(BF16),
                   ln_g[l, 0].reshape(1, d), ln_b[l, 0].reshape(1, d), alpha)
        if l % 2 == 0:
            x = _ffn_dense(x, ffn_w_gate_up[l // 2].astype(BF16), ffn_w_down[l // 2].astype(BF16),
                           ln_g[l, 1].reshape(1, d), ln_b[l, 1].reshape(1, d), alpha)
        else:
            x = _moe(x, moe_router[l // 2], moe_w_gate_up[l // 2].astype(BF16), moe_w_down[l // 2].astype(BF16),
                     ln_g[l, 1].reshape(1, d), ln_b[l, 1].reshape(1, d), alpha)

        outs["pk"].append(kp.reshape(bp, tp, DSA_KV_HEADS, DSA_HEAD_DIM))
        outs["pv"].append(vp.reshape(bp, tp, DSA_KV_HEADS, DSA_HEAD_DIM))
        outs["pki"].append(kip32.reshape(bp, tp, IDX_DIM))
        outs["ph"].append(sh_p)
        outs["pmk"].append(mem_kv[..., :MEM_WIDTH].reshape(bp, m, MEM_HEADS, MEM_HEAD_DIM))
        outs["pmv"].append(mem_kv[..., MEM_WIDTH:].reshape(bp, m, MEM_HEADS, MEM_HEAD_DIM))
        outs["sk"].append(ks.reshape(bs, ts, DSA_KV_HEADS, DSA_HEAD_DIM))
        outs["sv"].append(vs.reshape(bs, ts, DSA_KV_HEADS, DSA_HEAD_DIM))
        outs["ski"].append(kis32.reshape(bs, ts, IDX_DIM))
        outs["sh"].append(sh_s)

    st = {k: jnp.stack(v) for k, v in outs.items()}
    y_prompt = x[:np_rows].reshape(bp, tp, d)
    y_sample = x[np_rows:].reshape(bs, ts, d)
    return (y_prompt, y_sample, st["pk"], st["pv"], st["pki"], st["ph"], st["pmk"], st["pmv"],
            st["sk"], st["sv"], st["ski"], st["sh"])
```

```python
import functools
import math

import jax
import jax.numpy as jnp
from jax import lax
from jax.experimental import pallas as pl
from jax.experimental.pallas import tpu as pltpu

F32, BF16, I32 = jnp.float32, jnp.bfloat16, jnp.int32

D_MODEL = 1024
CHUNK = 64
HG_HEADS, HG_DK, HG_DV, HG_BLOCK = 4, 128, 128, 32
HG_WIDTH = HG_HEADS * HG_DV
LB_FLOOR = 1e-30
DSA_HEADS, DSA_KV_HEADS, DSA_HEAD_DIM = 8, 2, 64
DSA_GROUP = DSA_HEADS // DSA_KV_HEADS
DSA_WIDTH = DSA_HEADS * DSA_HEAD_DIM
IDX_HEADS, IDX_DIM = 8, 32
TOPK_MAX = 256
Q_BLOCK = 64
MEM_HEADS, MEM_HEAD_DIM = 4, 128
MEM_WIDTH = MEM_HEADS * MEM_HEAD_DIM
N_BRANCH = 3
N_EXPERTS, TOP_K = 8, 2
ROPE_THETA = 10000.0
LN_EPS = 1e-5
RMS_EPS = 1e-6
NEG_INF = -1e30
IN_SIZES = (HG_HEADS * HG_DK, HG_HEADS * HG_DK, HG_HEADS * HG_DV, HG_WIDTH,
            DSA_WIDTH, DSA_KV_HEADS * DSA_HEAD_DIM, DSA_KV_HEADS * DSA_HEAD_DIM,
            IDX_HEADS * IDX_DIM, IDX_DIM, IDX_HEADS, MEM_WIDTH, N_BRANCH * D_MODEL)

LANE = 128
SUBLANE = 8
VMEM_LIMIT = 48 * 1024 * 1024

C_HQ, C_HF, C_HI, C_HG = 0, 512, 1024, 1536
C_GATE = 2048
C_DQ = 5120
C_MQ = 5632
C_IQ = 6144
C_DK = 6400
C_DV = 6528
C_IK = 6656
H_COLS = 6912
IN_TN = 768

DSA_KEY_CHUNK = 256
BIS_PASSES_PER_CHECK = 4

NT_DIMS = (((1,), (1,)), ((), ()))


def _cparams(sem):
    return pltpu.CompilerParams(dimension_semantics=sem, vmem_limit_bytes=VMEM_LIMIT)


def _tile(n, pref, mult=SUBLANE):
    t = min(n, pref)
    while n % t or t % mult:
        t -= 1
    return t


def _sigmoid(x):
    return 1.0 / (1.0 + jnp.exp(-x))


def _layer_norm(v, g, b):
    mu = jnp.mean(v, axis=-1, keepdims=True)
    d = v - mu
    var = jnp.mean(d * d, axis=-1, keepdims=True)
    return d * lax.rsqrt(var + LN_EPS) * g + b


def _mm_kernel(x_ref, w_ref, o_ref):
    o_ref[...] = jnp.dot(x_ref[...].astype(BF16), w_ref[...], preferred_element_type=F32)


def _matmul(x, w, tm, tn, name):
    n, k = x.shape
    nc = w.shape[1]
    return pl.pallas_call(
        _mm_kernel,
        grid=(n // tm, nc // tn),
        in_specs=[pl.BlockSpec((tm, k), lambda i, j: (i, 0)),
                  pl.BlockSpec((k, tn), lambda i, j: (0, j))],
        out_specs=pl.BlockSpec((tm, tn), lambda i, j: (i, j)),
        out_shape=jax.ShapeDtypeStruct((n, nc), F32),
        compiler_params=_cparams(("parallel", "arbitrary")),
        name=name,
    )(x, w)


def _rot_half(x, half):
    lane = lax.broadcasted_iota(I32, x.shape, 1)
    lo = (lane % (2 * half)) < half
    return jnp.where(lo, pltpu.roll(x, LANE - half, 1), pltpu.roll(x, half, 1))


def _rope_kernel(dq_ref, dk_ref, iq_ref, ik_ref, c64_ref, s64_ref, c32_ref, s32_ref, cik_ref, sik_ref,
                 q_ref, k_ref, qi_ref, ki_ref):
    c64, s64 = c64_ref[...], s64_ref[...]
    c32, s32 = c32_ref[...], s32_ref[...]
    for j in range(DSA_WIDTH // LANE):
        x = dq_ref[:, j * LANE:(j + 1) * LANE]
        q_ref[:, j * LANE:(j + 1) * LANE] = x * c64 + _rot_half(x, DSA_HEAD_DIM // 2) * s64
    x = dk_ref[...]
    k_ref[...] = x * c64 + _rot_half(x, DSA_HEAD_DIM // 2) * s64
    for j in range(IDX_HEADS * IDX_DIM // LANE):
        x = iq_ref[:, j * LANE:(j + 1) * LANE]
        qi_ref[:, j * LANE:(j + 1) * LANE] = x * c32 + _rot_half(x, IDX_DIM // 2) * s32
    x = ik_ref[...]
    ki_ref[...] = x * cik_ref[...] + _rot_half(x, IDX_DIM // 2) * sik_ref[...]


def _rope_tables(pos):
    lane = jnp.arange(LANE)
    p = pos.astype(F32)[:, None]
    h64 = DSA_HEAD_DIM // 2
    inv64 = ROPE_THETA ** (-jnp.arange(h64, dtype=F32) / h64)
    a64 = p * inv64[lane % h64][None, :]
    c64 = jnp.cos(a64)
    s64 = jnp.sin(a64) * jnp.where((lane % DSA_HEAD_DIM) < h64, -1.0, 1.0)[None, :]
    h32 = IDX_DIM // 2
    inv32 = ROPE_THETA ** (-jnp.arange(h32, dtype=F32) / h32)
    a32 = p * inv32[lane % h32][None, :]
    c32 = jnp.cos(a32)
    s32 = jnp.sin(a32) * jnp.where((lane % IDX_DIM) < h32, -1.0, 1.0)[None, :]
    live = (lane < IDX_DIM)[None, :]
    cik = jnp.where(live, c32, 1.0)
    sik = jnp.where(live, s32, 0.0)
    return tuple(t.astype(F32) for t in (c64, s64, c32, s32, cik, sik))


def _rope(h, tables, row0, nb, t):
    tr = _tile(t, 256)
    nt = t // tr
    rb0 = row0 // tr
    rows = nb * t

    def hspec(width, col):
        return pl.BlockSpec((tr, width), lambda i: (rb0 + i, col // width))

    tspec = pl.BlockSpec((tr, LANE), lambda i: (i % nt, 0))

    def ospec(width):
        return pl.BlockSpec((tr, width), lambda i: (i, 0))

    return pl.pallas_call(
        _rope_kernel,
        grid=(rows // tr,),
        in_specs=[hspec(DSA_WIDTH, C_DQ), hspec(LANE, C_DK), hspec(IDX_HEADS * IDX_DIM, C_IQ), hspec(LANE, C_IK)]
        + [tspec] * 6,
        out_specs=[ospec(DSA_WIDTH), ospec(LANE), ospec(IDX_HEADS * IDX_DIM), ospec(LANE)],
        out_shape=[jax.ShapeDtypeStruct((rows, DSA_WIDTH), F32), jax.ShapeDtypeStruct((rows, LANE), F32),
                   jax.ShapeDtypeStruct((rows, IDX_HEADS * IDX_DIM), F32), jax.ShapeDtypeStruct((rows, LANE), F32)],
        compiler_params=_cparams(("parallel",)),
        name="rope",
    )(h, h, h, h, *tables)


def _hgrn_kernel(q_ref, f_ref, i_ref, g_ref, lb_ref, ng_ref, s0_ref, o_ref, s_ref,
                 st_s, b_s, k_s, v_s, q_s, o_s, *, tt):
    t = pl.program_id(2)
    c = HG_BLOCK

    @pl.when(t == 0)
    def _():
        st_s[...] = s0_ref[...].T

    lb = lb_ref[...]
    z = f_ref[...]
    lf = jnp.log(jnp.maximum(lb, LB_FLOOR) + (1.0 - lb) * _sigmoid(z))
    k_s[...] = (1.0 - lb) * _sigmoid(-z)
    hq = q_ref[...]
    q_s[...] = hq * _sigmoid(hq)
    v_s[...] = i_ref[...]

    r = lax.broadcasted_iota(I32, (tt, tt), 0)
    cc = lax.broadcasted_iota(I32, (tt, tt), 1)
    tri = jnp.where(((r // c) == (cc // c)) & (cc <= r), 1.0, 0.0).astype(BF16)
    hi = lf.astype(BF16)
    r1 = lf - hi.astype(F32)
    mid = r1.astype(BF16)
    lo = (r1 - mid.astype(F32)).astype(BF16)
    b_s[...] = (jnp.dot(tri, hi, preferred_element_type=F32) + jnp.dot(tri, mid, preferred_element_type=F32)
                + jnp.dot(tri, lo, preferred_element_type=F32))

    row8 = lax.broadcasted_iota(I32, (SUBLANE, LANE), 0)
    nsub = c // SUBLANE

    def block(j, carry):
        base = pl.multiple_of(j * c, c)
        bj = b_s[pl.ds(base, c), :]
        qj = q_s[pl.ds(base, c), :]
        kj = k_s[pl.ds(base, c), :]
        vj = v_s[pl.ds(base, c), :]
        blast = b_s[pl.ds(base + c - 1, 1), :]
        oc = [jnp.zeros((SUBLANE, LANE), F32) for _ in range(nsub)]
        for s in range(c):
            bs = b_s[pl.ds(base + s, 1), :]
            ks = k_s[pl.ds(base + s, 1), :]
            vs = v_s[pl.ds(base + s, 1), :]
            for u in range(s // SUBLANE, nsub):
                d = bj[u * SUBLANE:(u + 1) * SUBLANE] - bs
                if u == s // SUBLANE:
                    d = jnp.where(row8 >= (s % SUBLANE), d, NEG_INF)
                a = jnp.sum(qj[u * SUBLANE:(u + 1) * SUBLANE] * jnp.exp(d) * ks, axis=-1, keepdims=True)
                oc[u] = oc[u] + a * vs
        o_intra = jnp.concatenate(oc, axis=0)
        st = st_s[...]
        qe = qj * jnp.exp(bj)
        o_inter = lax.dot_general(qe.astype(BF16), st.astype(BF16), NT_DIMS, preferred_element_type=F32)
        ke = kj * jnp.exp(blast - bj)
        upd = jnp.dot(vj.T.astype(BF16), ke.astype(BF16), preferred_element_type=F32)
        st_s[...] = jnp.exp(blast) * st + upd
        o_s[pl.ds(base, c), :] = o_intra + o_inter
        return carry

    lax.fori_loop(0, tt // c, block, 0)

    o = o_s[...]
    o = o * lax.rsqrt(jnp.mean(o * o, axis=-1, keepdims=True) + RMS_EPS)
    o_ref[...] = o * ng_ref[...] * _sigmoid(g_ref[...])

    @pl.when(t == pl.num_programs(2) - 1)
    def _():
        s_ref[...] = st_s[...].T


def _hgrn(h, lb, ng, s0, row0, nb, t):
    assert t % HG_BLOCK == 0
    tt = _tile(t, 256, HG_BLOCK)
    nt = t // tt
    rb0 = row0 // tt

    def hspec(col):
        return pl.BlockSpec((tt, LANE), lambda b, hd, i: (rb0 + b * nt + i, col // LANE + hd))

    vspec = pl.BlockSpec((1, LANE), lambda b, hd, i: (0, hd))
    sspec = pl.BlockSpec((None, None, HG_DK, HG_DV), lambda b, hd, i: (b, hd, 0, 0))
    return pl.pallas_call(
        functools.partial(_hgrn_kernel, tt=tt),
        grid=(nb, HG_HEADS, nt),
        in_specs=[hspec(C_HQ), hspec(C_HF), hspec(C_HI), hspec(C_HG), vspec, vspec, sspec],
        out_specs=[pl.BlockSpec((tt, LANE), lambda b, hd, i: (b * nt + i, hd)), sspec],
        out_shape=[jax.ShapeDtypeStruct((nb * t, HG_WIDTH), F32),
                   jax.ShapeDtypeStruct((nb, HG_HEADS, HG_DK, HG_DV), F32)],
        scratch_shapes=[pltpu.VMEM((HG_DV, HG_DK), F32)] + [pltpu.VMEM((tt, LANE), F32)] * 5,
        compiler_params=_cparams(("parallel", "parallel", "arbitrary")),
        name="hgrn2",
    )(h, h, h, h, lb, ng, s0)


def _dsa_kernel(q_ref, qi_ref, w_ref, k_ref, v_ref, ki_ref, o_ref, kb_s, vt_s, kib_s, sc_s,
                *, past, nkeys, topk, kc, qblk, jbits):
    qb = pl.program_id(1)
    kf = float(topk)
    acc_rows = 4 * SUBLANE

    @pl.when(qb == 0)
    def _():
        kb_s[...] = k_ref[...].astype(BF16)
        kib_s[...] = ki_ref[...].astype(BF16)
        for c in range(sc_s.shape[0]):
            vt_s[c] = v_ref[c * kc:(c + 1) * kc, :].T.astype(BF16)

    qcol = lax.broadcasted_iota(I32, (1, qblk), 1)
    n_adm = jnp.minimum(((past + qb * qblk + qcol) // CHUNK + 1) * CHUNK, nkeys)
    n_adm_max = jnp.minimum(((past + (qb + 1) * qblk - 1) // CHUNK + 1) * CHUNK, nkeys)
    nkc = (n_adm_max + kc - 1) // kc
    qit = qi_ref[...].T
    wt = (w_ref[...] * (IDX_HEADS ** -0.5)).T
    qih = [qit[h * IDX_DIM:(h + 1) * IDX_DIM].astype(BF16) for h in range(IDX_HEADS)]
    wh = [wt[IDX_DIM + h:IDX_DIM + h + 1] for h in range(IDX_HEADS)]
    kpos = lax.broadcasted_iota(I32, (kc, qblk), 0)

    def score_chunk(c, carry):
        mn, mx = carry
        base = pl.multiple_of(c * kc, kc)
        kic = kib_s[pl.ds(base, kc), :]
        sc = jnp.zeros((kc, qblk), F32)
        for h in range(IDX_HEADS):
            rel = jnp.dot(kic, qih[h], preferred_element_type=F32)
            sc = sc + wh[h] * jnp.maximum(rel, 0.0)
        adm = (base + kpos) < n_adm
        sc_s[c] = jnp.where(adm, sc, NEG_INF)
        mn = jnp.minimum(mn, jnp.min(jnp.where(adm, sc, jnp.inf), axis=0, keepdims=True))
        mx = jnp.maximum(mx, jnp.max(jnp.where(adm, sc, -jnp.inf), axis=0, keepdims=True))
        return mn, mx

    mn, mx = lax.fori_loop(0, nkc, score_chunk,
                           (jnp.full((1, qblk), jnp.inf, F32), jnp.full((1, qblk), -jnp.inf, F32)))

    def count(pred):
        def body(c, acc):
            hit = jnp.where(pred(c, sc_s[c]), 1.0, 0.0)
            for i in range(kc // acc_rows):
                acc = acc + hit[i * acc_rows:(i + 1) * acc_rows]
            return acc
        acc = lax.fori_loop(0, nkc, body, jnp.zeros((acc_rows, qblk), F32))
        return jnp.sum(acc, axis=0, keepdims=True)

    cl0 = jnp.where(n_adm <= topk, kf, n_adm.astype(F32))
    hi0 = mx + (jnp.abs(mx) + 1.0)

    def mid_of(lo, hi):
        return lo + (hi - lo) * 0.5

    def bis_cond(cy):
        lo, hi, cl, ch, it = cy
        mid = mid_of(lo, hi)
        active = (cl != kf) & (mid > lo) & (mid < hi)
        return jnp.logical_and(it < 2200, jnp.max(jnp.where(active, 1.0, 0.0)) > 0.0)

    def bis_pass(lo, hi, cl, ch):
        mid = mid_of(lo, hi)
        cnt = count(lambda c, s: s >= mid)
        inside = (mid > lo) & (mid < hi)
        up = inside & (cnt >= kf)
        dn = inside & (cnt < kf)
        return jnp.where(up, mid, lo), jnp.where(dn, mid, hi), jnp.where(up, cnt, cl), jnp.where(dn, cnt, ch)

    def bis_body(cy):
        lo, hi, cl, ch, it = cy
        for _ in range(BIS_PASSES_PER_CHECK):
            lo, hi, cl, ch = bis_pass(lo, hi, cl, ch)
        return lo, hi, cl, ch, it + 1

    lo, hi, cl, ch, _ = lax.while_loop(bis_cond, bis_body, (mn, hi0, cl0, jnp.zeros((1, qblk), F32), jnp.int32(0)))

    need = kf - ch

    def tie_limit():
        def bit(i, j):
            jt = j + lax.shift_left(jnp.int32(1), jnp.asarray(jbits - 1 - i, I32))
            cnt = count(lambda c, s: (s >= lo) & (s < hi) & ((c * kc + kpos) < jt))
            return jnp.where(cnt <= need, jt, j)
        return lax.fori_loop(0, jbits, bit, jnp.zeros((1, qblk), I32))

    has_tie = jnp.max(jnp.where(cl > kf, 1.0, 0.0)) > 0.0
    jl = lax.cond(has_tie, tie_limit, lambda: jnp.full((1, qblk), 2 ** 30, I32))

    qt = (q_ref[...] * (DSA_HEAD_DIM ** -0.5)).T
    qg = [jnp.concatenate([qt[(n * DSA_GROUP + r) * DSA_HEAD_DIM:(n * DSA_GROUP + r + 1) * DSA_HEAD_DIM]
                           for r in range(DSA_GROUP)], axis=1).astype(BF16) for n in range(DSA_KV_HEADS)]
    cols = DSA_GROUP * qblk

    def att_chunk(c, carry):
        base = pl.multiple_of(c * kc, kc)
        s = sc_s[c]
        sel = ((s >= hi) | ((s >= lo) & ((base + kpos) < jl))) & (s > 0.5 * NEG_INF)
        self = jnp.where(sel, 1.0, 0.0)
        sel4 = jnp.concatenate([self] * DSA_GROUP, axis=1) > 0.5
        kch = kb_s[pl.ds(base, kc), :]
        out = []
        for n in range(DSA_KV_HEADS):
            m, l, acc = carry[n]
            kn = kch[:, n * DSA_HEAD_DIM:(n + 1) * DSA_HEAD_DIM]
            vtn = vt_s[c, n * DSA_HEAD_DIM:(n + 1) * DSA_HEAD_DIM, :]
            lg = jnp.dot(kn, qg[n], preferred_element_type=F32)
            lg = jnp.where(sel4, lg, NEG_INF)
            m_new = jnp.maximum(m, jnp.max(lg, axis=0, keepdims=True))
            alpha = jnp.exp(m - m_new)
            p = jnp.exp(lg - m_new)
            l = alpha * l + jnp.sum(p, axis=0, keepdims=True)
            acc = alpha * acc + jnp.dot(vtn, p.astype(BF16), preferred_element_type=F32)
            out.append((m_new, l, acc))
        return tuple(out)

    init = tuple((jnp.full((1, cols), NEG_INF, F32), jnp.zeros((1, cols), F32), jnp.zeros((DSA_HEAD_DIM, cols), F32))
                 for _ in range(DSA_KV_HEADS))
    res = lax.fori_loop(0, nkc, att_chunk, init)
    pieces = []
    for n in range(DSA_KV_HEADS):
        _, l, acc = res[n]
        og = acc / l
        pieces += [og[:, r * qblk:(r + 1) * qblk] for r in range(DSA_GROUP)]
    o_ref[...] = jnp.concatenate(pieces, axis=0).T


def _dsa(q, qi, kiw, k_all, v_all, ki_all, nb, t, past, topk):
    lp = k_all.shape[1]
    kc = DSA_KEY_CHUNK
    qblk = min(t, LANE)
    assert lp % kc == 0 and kc >= topk and t % qblk == 0 and qblk % CHUNK == 0
    nq = t // qblk
    jbits = int(math.ceil(math.log2(lp))) + 1

    def qspec(width):
        return pl.BlockSpec((qblk, width), lambda b, i: (b * nq + i, 0))

    def kspec(width):
        return pl.BlockSpec((None, lp, width), lambda b, i: (b, 0, 0))

    return pl.pallas_call(
        functools.partial(_dsa_kernel, past=past, nkeys=past + t, topk=topk, kc=kc, qblk=qblk, jbits=jbits),
        grid=(nb, nq),
        in_specs=[qspec(DSA_WIDTH), qspec(IDX_HEADS * IDX_DIM), qspec(LANE), kspec(LANE), kspec(LANE), kspec(IDX_DIM)],
        out_specs=qspec(DSA_WIDTH),
        out_shape=jax.ShapeDtypeStruct((nb * t, DSA_WIDTH), F32),
        scratch_shapes=[pltpu.VMEM((lp, LANE), BF16), pltpu.VMEM((lp // kc, LANE, kc), BF16),
                        pltpu.VMEM((lp, IDX_DIM), BF16), pltpu.VMEM((lp // kc, kc, qblk), F32)],
        compiler_params=_cparams(("parallel", "arbitrary")),
        name="dsa",
    )(q, qi, kiw, k_all, v_all, ki_all)


def _mem_kernel(q_ref, mk_ref, mv_ref, o_ref):
    scale = MEM_HEAD_DIM ** -0.5
    for h in range(MEM_HEADS):
        sl = slice(h * MEM_HEAD_DIM, (h + 1) * MEM_HEAD_DIM)
        qh = q_ref[:, sl].astype(BF16)
        kh = mk_ref[:, sl].astype(BF16)
        vh = mv_ref[:, sl].astype(BF16)
        lg = lax.dot_general(qh, kh, NT_DIMS, preferred_element_type=F32) * scale
        p = jnp.exp(lg - jnp.max(lg, axis=-1, keepdims=True))
        o = jnp.dot(p.astype(BF16), vh, preferred_element_type=F32)
        o_ref[:, sl] = o / jnp.sum(p, axis=-1, keepdims=True)


def _mem_attn(h, mk, mv, kcol, vcol, row0, nb, t):
    tq = _tile(t, 256)
    nt = t // tq
    rb0 = row0 // tq
    m = mk.shape[1]
    return pl.pallas_call(
        _mem_kernel,
        grid=(nb, nt),
        in_specs=[pl.BlockSpec((tq, MEM_WIDTH), lambda b, i: (rb0 + b * nt + i, C_MQ // MEM_WIDTH)),
                  pl.BlockSpec((None, m, MEM_WIDTH), lambda b, i: (b, 0, kcol)),
                  pl.BlockSpec((None, m, MEM_WIDTH), lambda b, i: (b, 0, vcol))],
        out_specs=pl.BlockSpec((tq, MEM_WIDTH), lambda b, i: (b * nt + i, 0)),
        out_shape=jax.ShapeDtypeStruct((nb * t, MEM_WIDTH), F32),
        compiler_params=_cparams(("parallel", "parallel")),
        name="mem_attn",
    )(h, mk, mv)


def _merge_kernel(bh_ref, bd_ref, bm_ref, g0_ref, g1_ref, g2_ref, x_ref, wb_ref, wo_ref, lg_ref, lb_ref, o_ref,
                  *, alpha):
    merged = None
    for br_ref, g_ref, n in ((bh_ref, g0_ref, 0), (bd_ref, g1_ref, 1), (bm_ref, g2_ref, 2)):
        proj = jnp.dot(br_ref[...].astype(BF16), wb_ref[n], preferred_element_type=F32)
        term = _sigmoid(g_ref[...]) * proj
        merged = term if merged is None else merged + term
    m = jnp.dot(merged.astype(BF16), wo_ref[...], preferred_element_type=F32)
    o_ref[...] = _layer_norm(alpha * x_ref[...] + m, lg_ref[...], lb_ref[...])


def _merge(o_hg, o_dsa, o_mem, h, x, wb, wo, lng, lnb, alpha):
    n = x.shape[0]
    tm = _tile(n, 256)
    bspec = pl.BlockSpec((tm, 512), lambda i: (i, 0))

    def gspec(j):
        return pl.BlockSpec((tm, D_MODEL), lambda i: (i, C_GATE // D_MODEL + j))

    xspec = pl.BlockSpec((tm, D_MODEL), lambda i: (i, 0))
    vspec = pl.BlockSpec((1, D_MODEL), lambda i: (0, 0))
    return pl.pallas_call(
        functools.partial(_merge_kernel, alpha=alpha),
        grid=(n // tm,),
        in_specs=[bspec, bspec, bspec, gspec(0), gspec(1), gspec(2), xspec,
                  pl.BlockSpec((N_BRANCH, 512, D_MODEL), lambda i: (0, 0, 0)),
                  pl.BlockSpec((D_MODEL, D_MODEL), lambda i: (0, 0)), vspec, vspec],
        out_specs=xspec,
        out_shape=jax.ShapeDtypeStruct((n, D_MODEL), F32),
        compiler_params=_cparams(("parallel",)),
        name="merge",
    )(o_hg, o_dsa, o_mem, h, h, h, x, wb, wo, lng, lnb)


def _swiglu_partial(xb, wg_ref, wu_ref, wd_ref):
    g = jnp.dot(xb, wg_ref[...], preferred_element_type=F32)
    u = jnp.dot(xb, wu_ref[...], preferred_element_type=F32)
    a = (g * _sigmoid(g)) * u
    return jnp.dot(a.astype(BF16), wd_ref[...], preferred_element_type=F32)


def _ffn_kernel(x_ref, wg_ref, wu_ref, wd_ref, lg_ref, lb_ref, o_ref, xb_s, acc_s, *, alpha):
    k = pl.program_id(1)

    @pl.when(k == 0)
    def _():
        xb_s[...] = x_ref[...].astype(BF16)
        acc_s[...] = jnp.zeros_like(acc_s)

    acc_s[...] += _swiglu_partial(xb_s[...], wg_ref, wu_ref, wd_ref)

    @pl.when(k == pl.num_programs(1) - 1)
    def _():
        o_ref[...] = _layer_norm(alpha * x_ref[...] + acc_s[...], lg_ref[...], lb_ref[...])


def _ffn_dense(x, w_gu, w_down, lng, lnb, alpha):
    n = x.shape[0]
    f = w_down.shape[0]
    tm = _tile(n, 512)
    fc = 256
    nk = f // fc
    xspec = pl.BlockSpec((tm, D_MODEL), lambda i, k: (i, 0))
    vspec = pl.BlockSpec((1, D_MODEL), lambda i, k: (0, 0))
    return pl.pallas_call(
        functools.partial(_ffn_kernel, alpha=alpha),
        grid=(n // tm, nk),
        in_specs=[xspec,
                  pl.BlockSpec((D_MODEL, fc), lambda i, k: (0, k)),
                  pl.BlockSpec((D_MODEL, fc), lambda i, k: (0, nk + k)),
                  pl.BlockSpec((fc, D_MODEL), lambda i, k: (k, 0)), vspec, vspec],
        out_specs=xspec,
        out_shape=jax.ShapeDtypeStruct((n, D_MODEL), F32),
        scratch_shapes=[pltpu.VMEM((tm, D_MODEL), BF16), pltpu.VMEM((tm, D_MODEL), F32)],
        compiler_params=_cparams(("parallel", "arbitrary")),
        name="ffn_dense",
    )(x, w_gu, w_gu, w_down, lng, lnb)


def _router_kernel(x_ref, w_ref, e_ref, g_ref):
    x = x_ref[...]
    w = w_ref[...]
    xh = x.astype(BF16)
    xl = (x - xh.astype(F32)).astype(BF16)
    wh = w.astype(BF16)
    wl = (w - wh.astype(F32)).astype(BF16)
    lg = (jnp.dot(xh, wh, preferred_element_type=F32) + jnp.dot(xh, wl, preferred_element_type=F32)
          + jnp.dot(xl, wh, preferred_element_type=F32))
    lane = lax.broadcasted_iota(I32, lg.shape, 1).astype(F32)
    lg = jnp.where(lane < N_EXPERTS, lg, -jnp.inf)
    m1 = jnp.max(lg, axis=-1, keepdims=True)
    i1 = jnp.min(jnp.where(lg == m1, lane, float(LANE)), axis=-1, keepdims=True)
    lg2 = jnp.where(lane == i1, -jnp.inf, lg)
    m2 = jnp.max(lg2, axis=-1, keepdims=True)
    i2 = jnp.min(jnp.where(lg2 == m2, lane, float(LANE)), axis=-1, keepdims=True)
    e = jnp.exp(m2 - m1)
    g1 = 1.0 / (1.0 + e)
    g2 = e / (1.0 + e)
    e_ref[...] = jnp.where(lane == 0.0, i1, jnp.where(lane == 1.0, i2, 0.0)).astype(I32)
    g_ref[...] = jnp.where(lane == 0.0, g1, jnp.where(lane == 1.0, g2, 0.0))


def _router(x, w_router):
    n = x.shape[0]
    tm = _tile(n, 512)
    wp = jnp.pad(w_router, ((0, 0), (0, LANE - N_EXPERTS)))
    ospec = pl.BlockSpec((tm, LANE), lambda i: (i, 0))
    return pl.pallas_call(
        _router_kernel,
        grid=(n // tm,),
        in_specs=[pl.BlockSpec((tm, D_MODEL), lambda i: (i, 0)), pl.BlockSpec((D_MODEL, LANE), lambda i: (0, 0))],
        out_specs=[ospec, ospec],
        out_shape=[jax.ShapeDtypeStruct((n, LANE), I32), jax.ShapeDtypeStruct((n, LANE), F32)],
        compiler_params=_cparams(("parallel",)),
        name="router",
    )(x, wp)


def _row_copy(src, dst, s, d, sem):
    return pltpu.make_async_copy(src.at[pl.ds(s, 1)], dst.at[pl.ds(d, 1)], sem)


def _dispatch_kernel(pos_ref, x_ref, xs_in, xs_hbm, sem, *, tb):
    del xs_in
    i = pl.program_id(0)

    def start(r, c):
        t = i * tb + r
        for j in range(TOP_K):
            _row_copy(x_ref, xs_hbm, r, pos_ref[TOP_K * t + j], sem).start()
        return c

    lax.fori_loop(0, tb, start, 0)

    def wait(r, c):
        for j in range(TOP_K):
            _row_copy(x_ref, xs_hbm, 0, 0, sem).wait()
        return c

    lax.fori_loop(0, tb, wait, 0)


def _dispatch(pos, x, n_rows):
    n = x.shape[0]
    tb = _tile(n, 256)
    xs0 = jnp.zeros((n_rows, D_MODEL), F32)
    return pl.pallas_call(
        functools.partial(_dispatch_kernel, tb=tb),
        grid_spec=pltpu.PrefetchScalarGridSpec(
            num_scalar_prefetch=1, grid=(n // tb,),
            in_specs=[pl.BlockSpec((tb, D_MODEL), lambda i, p: (i, 0)), pl.BlockSpec(memory_space=pl.ANY)],
            out_specs=pl.BlockSpec(memory_space=pl.ANY),
            scratch_shapes=[pltpu.SemaphoreType.DMA(())]),
        out_shape=jax.ShapeDtypeStruct((n_rows, D_MODEL), F32),
        input_output_aliases={2: 0},
        compiler_params=pltpu.CompilerParams(dimension_semantics=("arbitrary",), has_side_effects=True),
        name="moe_dispatch",
    )(pos, x, xs0)


def _expert_kernel(te_ref, nu_ref, x_ref, wg_ref, wu_ref, wd_ref, o_ref, xb_s, acc_s):
    del te_ref
    i = pl.program_id(0)
    k = pl.program_id(1)
    used = i < nu_ref[0]

    @pl.when(jnp.logical_and(used, k == 0))
    def _():
        xb_s[...] = x_ref[...].astype(BF16)
        acc_s[...] = jnp.zeros_like(acc_s)

    @pl.when(used)
    def _():
        acc_s[...] += _swiglu_partial(xb_s[...], wg_ref, wu_ref, wd_ref)

    @pl.when(k == pl.num_programs(1) - 1)
    def _():
        o_ref[...] = jnp.where(used, acc_s[...], 0.0)


def _experts(tile_expert, n_used, xs, w_gu, w_down, tm):
    n_rows = xs.shape[0]
    f = w_down.shape[1]
    fc = 512
    nk = f // fc
    xspec = pl.BlockSpec((tm, D_MODEL), lambda i, k, te, nu: (i, 0))
    return pl.pallas_call(
        _expert_kernel,
        grid_spec=pltpu.PrefetchScalarGridSpec(
            num_scalar_prefetch=2, grid=(n_rows // tm, nk),
            in_specs=[xspec,
                      pl.BlockSpec((None, D_MODEL, fc), lambda i, k, te, nu: (te[i], 0, k)),
                      pl.BlockSpec((None, D_MODEL, fc), lambda i, k, te, nu: (te[i], 0, nk + k)),
                      pl.BlockSpec((None, fc, D_MODEL), lambda i, k, te, nu: (te[i], k, 0))],
            out_specs=xspec,
            scratch_shapes=[pltpu.VMEM((tm, D_MODEL), BF16), pltpu.VMEM((tm, D_MODEL), F32)]),
        out_shape=jax.ShapeDtypeStruct((n_rows, D_MODEL), F32),
        compiler_params=_cparams(("arbitrary", "arbitrary")),
        name="moe_experts",
    )(tile_expert, n_used, xs, w_gu, w_gu, w_down)


def _combine_kernel(pos_ref, ys_hbm, g_ref, x_ref, lg_ref, lb_ref, o_ref, buf, sem, *, tb, alpha):
    i = pl.program_id(0)

    def start(r, c):
        t = i * tb + r
        for j in range(TOP_K):
            _row_copy(ys_hbm, buf.at[j], pos_ref[TOP_K * t + j], r, sem.at[j]).start()
        return c

    lax.fori_loop(0, tb, start, 0)

    def wait(r, c):
        for j in range(TOP_K):
            _row_copy(ys_hbm, buf.at[j], 0, 0, sem.at[j]).wait()
        return c

    lax.fori_loop(0, tb, wait, 0)
    g = g_ref[...]
    f = g[:, 0:1] * buf[0] + g[:, 1:2] * buf[1]
    o_ref[...] = _layer_norm(alpha * x_ref[...] + f, lg_ref[...], lb_ref[...])


def _combine(pos, ys, gates, x, lng, lnb, alpha):
    n = x.shape[0]
    tb = _tile(n, 256)
    xspec = pl.BlockSpec((tb, D_MODEL), lambda i, p: (i, 0))
    vspec = pl.BlockSpec((1, D_MODEL), lambda i, p: (0, 0))
    return pl.pallas_call(
        functools.partial(_combine_kernel, tb=tb, alpha=alpha),
        grid_spec=pltpu.PrefetchScalarGridSpec(
            num_scalar_prefetch=1, grid=(n // tb,),
            in_specs=[pl.BlockSpec(memory_space=pl.ANY), pl.BlockSpec((tb, LANE), lambda i, p: (i, 0)),
                      xspec, vspec, vspec],
            out_specs=xspec,
            scratch_shapes=[pltpu.VMEM((TOP_K, tb, D_MODEL), F32), pltpu.SemaphoreType.DMA((TOP_K,))]),
        out_shape=jax.ShapeDtypeStruct((n, D_MODEL), F32),
        compiler_params=_cparams(("arbitrary",)),
        name="moe_combine",
    )(pos, ys, gates, x, lng, lnb)


def _moe(x, w_router, w_gu, w_down, lng, lnb, alpha):
    n = x.shape[0]
    tm = _tile(n, 512)
    eidx, gates = _router(x, w_router)
    e2 = eidx[:, :TOP_K].reshape(-1)
    onehot = (e2[:, None] == jnp.arange(N_EXPERTS, dtype=I32)[None, :]).astype(I32)
    csum = jnp.cumsum(onehot, axis=0)
    rank = jnp.take_along_axis(csum, e2[:, None], axis=1)[:, 0] - 1
    counts = csum[-1]
    padded = ((counts + tm - 1) // tm) * tm
    ends = jnp.cumsum(padded)
    pos = ((ends - padded)[e2] + rank).astype(I32)
    n_rows = TOP_K * n + N_EXPERTS * tm
    n_tiles = n_rows // tm
    tile_start = jnp.arange(n_tiles, dtype=I32) * tm
    tile_expert = jnp.minimum(jnp.sum((tile_start[:, None] >= ends[None, :]).astype(I32), axis=1),
                              N_EXPERTS - 1).astype(I32)
    n_used = (ends[-1:] // tm).astype(I32)
    xs = _dispatch(pos, x, n_rows)
    ys = _experts(tile_expert, n_used, xs, w_gu, w_down, tm)
    return _combine(pos, ys, gates, x, lng, lnb, alpha)


def _pack_w_in(w):
    d = w.shape[0]
    parts, start = [], 0
    for n in IN_SIZES:
        parts.append(w[:, start:start + n])
        start += n
    hq, hf, hi, hg, dq, dk, dv, iq, ik, iw, mq, gates = parts
    ikw = jnp.concatenate([ik, iw, jnp.zeros((d, LANE - IDX_DIM - IDX_HEADS), w.dtype)], axis=1)
    tail = jnp.zeros((d, H_COLS - (C_IK + LANE)), w.dtype)
    return jnp.concatenate([hq, hf, hi, hg, gates, dq, mq, iq, dk, dv, ikw, tail], axis=1).astype(BF16)


def kernel(x_prompt, x_sample, cache_dsa_k, cache_dsa_v, cache_idx_k, state_hgrn, cache_mem_k, cache_mem_v,
           mem_prompt, w_in, hgrn_lb_logits, hgrn_norm_g, w_branch, w_out, w_mem_kv, ln_g, ln_b, ffn_w_gate_up,
           ffn_w_down, moe_router, moe_w_gate_up, moe_w_down):
    depth = w_in.shape[0]
    alpha = (2 * depth) ** 0.25
    bp, tp, d = x_prompt.shape
    bs, ts, _ = x_sample.shape
    past = cache_dsa_k.shape[2]
    m = mem_prompt.shape[1]
    np_rows, ns_rows = bp * tp, bs * ts

    sm = jax.nn.softmax(hgrn_lb_logits.astype(F32), axis=0)
    lbs = jnp.cumsum(sm, axis=0) - sm[0:1]

    x = jnp.concatenate([x_prompt.reshape(np_rows, d), x_sample.reshape(ns_rows, d)], axis=0)
    n = x.shape[0]
    tables_p = _rope_tables(jnp.arange(tp, dtype=I32))
    tables_s = _rope_tables(past + jnp.arange(ts, dtype=I32))
    mem_flat = mem_prompt.reshape(bp * m, d)
    zeros_state = jnp.zeros((bp, HG_HEADS, HG_DK, HG_DV), F32)
    kc = DSA_KEY_CHUNK
    lp_s = ((past + ts + kc - 1) // kc) * kc

    outs = {k: [] for k in ("pk", "pv", "pki", "ph", "pmk", "pmv", "sk", "sv", "ski", "sh")}
    for l in range(depth):
        h = _matmul(x, _pack_w_in(w_in[l]), _tile(n, 512), IN_TN, "in_proj")
        mem_kv = _matmul(mem_flat, w_mem_kv[l].astype(BF16), _tile(bp * m, 512), 512, "mem_kv").reshape(bp, m, 2 * MEM_WIDTH)
        lb = lbs[l].reshape(1, HG_WIDTH)
        ng = hgrn_norm_g[l].reshape(1, HG_WIDTH).astype(F32)

        qp, kp, qip, kip = _rope(h, tables_p, 0, bp, tp)
        vp = h[:np_rows, C_DV:C_DV + LANE]
        kip32 = kip[:, :IDX_DIM]
        ohg_p, sh_p = _hgrn(h, lb, ng, zeros_state, 0, bp, tp)
        odsa_p = _dsa(qp, qip, kip, kp.reshape(bp, tp, LANE), vp.reshape(bp, tp, LANE), kip32.reshape(bp, tp, IDX_DIM),
                      bp, tp, 0, min(TOPK_MAX, tp // 4))
        omem_p = _mem_attn(h, mem_kv, mem_kv, 0, 1, 0, bp, tp)

        qs, ks, qis, kis = _rope(h, tables_s, np_rows, bs, ts)
        vs = h[np_rows:, C_DV:C_DV + LANE]
        kis32 = kis[:, :IDX_DIM]
        npad = lp_s - past - ts
        k_all = jnp.concatenate([cache_dsa_k[l].reshape(bs, past, LANE), ks.reshape(bs, ts, LANE),
                                 jnp.zeros((bs, npad, LANE), F32)], axis=1)
        v_all = jnp.concatenate([cache_dsa_v[l].reshape(bs, past, LANE), vs.reshape(bs, ts, LANE),
                                 jnp.zeros((bs, npad, LANE), F32)], axis=1)
        ki_all = jnp.concatenate([cache_idx_k[l], kis32.reshape(bs, ts, IDX_DIM),
                                  jnp.zeros((bs, npad, IDX_DIM), F32)], axis=1)
        ohg_s, sh_s = _hgrn(h, lb, ng, state_hgrn[l], np_rows, bs, ts)
        odsa_s = _dsa(qs, qis, kis, k_all, v_all, ki_all, bs, ts, past, min(TOPK_MAX, (past + ts) // 4))
        omem_s = _mem_attn(h, cache_mem_k[l].reshape(bs, m, MEM_WIDTH), cache_mem_v[l].reshape(bs, m, MEM_WIDTH),
                           0, 0, np_rows, bs, ts)

        o_hg = jnp.concatenate([ohg_p, ohg_s], axis=0)
        o_dsa = jnp.concatenate([odsa_p, odsa_s], axis=0)
        o_mem = jnp.concatenate([omem_p, omem_s], axis=0)
        x = _merge(o_hg, o_dsa, o_mem, h, x, w_branch[l].astype(BF16), w_out[l].astype(BF16),
                   ln_g[l, 0].reshape(1, d), ln_b[l, 0].reshape(1, d), alpha)
        if l % 2 == 0:
            x = _ffn_dense(x, ffn_w_gate_up[l // 2].astype(BF16), ffn_w_down[l // 2].astype(BF16),
                           ln_g[l, 1].reshape(1, d), ln_b[l, 1].reshape(1, d), alpha)
        else:
            x = _moe(x, moe_router[l // 2], moe_w_gate_up[l // 2].astype(BF16), moe_w_down[l // 2].astype(BF16),
                     ln_g[l, 1].reshape(1, d), ln_b[l, 1].reshape(1, d), alpha)

        outs["pk"].append(kp.reshape(bp, tp, DSA_KV_HEADS, DSA_HEAD_DIM))
        outs["pv"].append(vp.reshape(bp, tp, DSA_KV_HEADS, DSA_HEAD_DIM))
        outs["pki"].append(kip32.reshape(bp, tp, IDX_DIM))
        outs["ph"].append(sh_p)
        outs["pmk"].append(mem_kv[..., :MEM_WIDTH].reshape(bp, m, MEM_HEADS, MEM_HEAD_DIM))
        outs["pmv"].append(mem_kv[..., MEM_WIDTH:].reshape(bp, m, MEM_HEADS, MEM_HEAD_DIM))
        outs["sk"].append(ks.reshape(bs, ts, DSA_KV_HEADS, DSA_HEAD_DIM))
        outs["sv"].append(vs.reshape(bs, ts, DSA_KV_HEADS, DSA_HEAD_DIM))
        outs["ski"].append(kis32.reshape(bs, ts, IDX_DIM))
        outs["sh"].append(sh_s)

    st = {k: jnp.stack(v) for k, v in outs.items()}
    y_prompt = x[:np_rows].reshape(bp, tp, d)
    y_sample = x[np_rows:].reshape(bs, ts, d)
    return (y_prompt, y_sample, st["pk"], st["pv"], st["pki"], st["ph"], st["pmk"], st["pmv"],
            st["sk"], st["sv"], st["ski"], st["sh"])
```

```python
import functools
import math

import jax
import jax.numpy as jnp
from jax import lax
from jax.experimental import pallas as pl
from jax.experimental.pallas import tpu as pltpu

F32, BF16, I32 = jnp.float32, jnp.bfloat16, jnp.int32

D_MODEL = 1024
CHUNK = 64
HG_HEADS, HG_DK, HG_DV, HG_BLOCK = 4, 128, 128, 32
HG_WIDTH = HG_HEADS * HG_DV
LB_FLOOR = 1e-30
DSA_HEADS, DSA_KV_HEADS, DSA_HEAD_DIM = 8, 2, 64
DSA_GROUP = DSA_HEADS // DSA_KV_HEADS
DSA_WIDTH = DSA_HEADS * DSA_HEAD_DIM
IDX_HEADS, IDX_DIM = 8, 32
TOPK_MAX = 256
Q_BLOCK = 64
MEM_HEADS, MEM_HEAD_DIM = 4, 128
MEM_WIDTH = MEM_HEADS * MEM_HEAD_DIM
N_BRANCH = 3
N_EXPERTS, TOP_K = 8, 2
ROPE_THETA = 10000.0
LN_EPS = 1e-5
RMS_EPS = 1e-6
NEG_INF = -1e30
IN_SIZES = (HG_HEADS * HG_DK, HG_HEADS * HG_DK, HG_HEADS * HG_DV, HG_WIDTH,
            DSA_WIDTH, DSA_KV_HEADS * DSA_HEAD_DIM, DSA_KV_HEADS * DSA_HEAD_DIM,
            IDX_HEADS * IDX_DIM, IDX_DIM, IDX_HEADS, MEM_WIDTH, N_BRANCH * D_MODEL)

LANE = 128
SUBLANE = 8
VMEM_LIMIT = 48 * 1024 * 1024

C_HQ, C_HF, C_HI, C_HG = 0, 512, 1024, 1536
C_GATE = 2048
C_DQ = 5120
C_MQ = 5632
C_IQ = 6144
C_DK = 6400
C_DV = 6528
C_IK = 6656
H_COLS = 6912
IN_TN = 768

DSA_KEY_CHUNK = 256
BIS_PASSES_PER_CHECK = 4

NT_DIMS = (((1,), (1,)), ((), ()))


def _cparams(sem):
    return pltpu.CompilerParams(dimension_semantics=sem, vmem_limit_bytes=VMEM_LIMIT)


def _tile(n, pref, mult=SUBLANE):
    t = min(n, pref)
    while n % t or t % mult:
        t -= 1
    return t


def _sigmoid(x):
    return 1.0 / (1.0 + jnp.exp(-x))


def _layer_norm(v, g, b):
    mu = jnp.mean(v, axis=-1, keepdims=True)
    d = v - mu
    var = jnp.mean(d * d, axis=-1, keepdims=True)
    return d * lax.rsqrt(var + LN_EPS) * g + b


def _mm_kernel(x_ref, w_ref, o_ref):
    o_ref[...] = jnp.dot(x_ref[...].astype(BF16), w_ref[...], preferred_element_type=F32)


def _matmul(x, w, tm, tn, name):
    n, k = x.shape
    nc = w.shape[1]
    return pl.pallas_call(
        _mm_kernel,
        grid=(n // tm, nc // tn),
        in_specs=[pl.BlockSpec((tm, k), lambda i, j: (i, 0)),
                  pl.BlockSpec((k, tn), lambda i, j: (0, j))],
        out_specs=pl.BlockSpec((tm, tn), lambda i, j: (i, j)),
        out_shape=jax.ShapeDtypeStruct((n, nc), F32),
        compiler_params=_cparams(("parallel", "arbitrary")),
        name=name,
    )(x, w)


def _rot_half(x, half):
    lane = lax.broadcasted_iota(I32, x.shape, 1)
    lo = (lane % (2 * half)) < half
    return jnp.where(lo, pltpu.roll(x, LANE - half, 1), pltpu.roll(x, half, 1))


def _rope_kernel(dq_ref, dk_ref, iq_ref, ik_ref, c64_ref, s64_ref, c32_ref, s32_ref, cik_ref, sik_ref,
                 q_ref, k_ref, qi_ref, ki_ref):
    c64, s64 = c64_ref[...], s64_ref[...]
    c32, s32 = c32_ref[...], s32_ref[...]
    for j in range(DSA_WIDTH // LANE):
        x = dq_ref[:, j * LANE:(j + 1) * LANE]
        q_ref[:, j * LANE:(j + 1) * LANE] = x * c64 + _rot_half(x, DSA_HEAD_DIM // 2) * s64
    x = dk_ref[...]
    k_ref[...] = x * c64 + _rot_half(x, DSA_HEAD_DIM // 2) * s64
    for j in range(IDX_HEADS * IDX_DIM // LANE):
        x = iq_ref[:, j * LANE:(j + 1) * LANE]
        qi_ref[:, j * LANE:(j + 1) * LANE] = x * c32 + _rot_half(x, IDX_DIM // 2) * s32
    x = ik_ref[...]
    ki_ref[...] = x * cik_ref[...] + _rot_half(x, IDX_DIM // 2) * sik_ref[...]


def _rope_tables(pos):
    lane = jnp.arange(LANE)
    p = pos.astype(F32)[:, None]
    h64 = DSA_HEAD_DIM // 2
    inv64 = ROPE_THETA ** (-jnp.arange(h64, dtype=F32) / h64)
    a64 = p * inv64[lane % h64][None, :]
    c64 = jnp.cos(a64)
    s64 = jnp.sin(a64) * jnp.where((lane % DSA_HEAD_DIM) < h64, -1.0, 1.0)[None, :]
    h32 = IDX_DIM // 2
    inv32 = ROPE_THETA ** (-jnp.arange(h32, dtype=F32) / h32)
    a32 = p * inv32[lane % h32][None, :]
    c32 = jnp.cos(a32)
    s32 = jnp.sin(a32) * jnp.where((lane % IDX_DIM) < h32, -1.0, 1.0)[None, :]
    live = (lane < IDX_DIM)[None, :]
    cik = jnp.where(live, c32, 1.0)
    sik = jnp.where(live, s32, 0.0)
    return tuple(t.astype(F32) for t in (c64, s64, c32, s32, cik, sik))


def _rope(h, tables, row0, nb, t):
    tr = _tile(t, 256)
    nt = t // tr
    rb0 = row0 // tr
    rows = nb * t

    def hspec(width, col):
        return pl.BlockSpec((tr, width), lambda i: (rb0 + i, col // width))

    tspec = pl.BlockSpec((tr, LANE), lambda i: (i % nt, 0))

    def ospec(width):
        return pl.BlockSpec((tr, width), lambda i: (i, 0))

    return pl.pallas_call(
        _rope_kernel,
        grid=(rows // tr,),
        in_specs=[hspec(DSA_WIDTH, C_DQ), hspec(LANE, C_DK), hspec(IDX_HEADS * IDX_DIM, C_IQ), hspec(LANE, C_IK)]
        + [tspec] * 6,
        out_specs=[ospec(DSA_WIDTH), ospec(LANE), ospec(IDX_HEADS * IDX_DIM), ospec(LANE)],
        out_shape=[jax.ShapeDtypeStruct((rows, DSA_WIDTH), F32), jax.ShapeDtypeStruct((rows, LANE), F32),
                   jax.ShapeDtypeStruct((rows, IDX_HEADS * IDX_DIM), F32), jax.ShapeDtypeStruct((rows, LANE), F32)],
        compiler_params=_cparams(("parallel",)),
        name="rope",
    )(h, h, h, h, *tables)


def _hgrn_kernel(q_ref, f_ref, i_ref, g_ref, lb_ref, ng_ref, s0_ref, o_ref, s_ref,
                 st_s, b_s, k_s, v_s, q_s, o_s, *, tt):
    t = pl.program_id(1)
    c = HG_BLOCK

    @pl.when(t == 0)
    def _():
        for hd in range(HG_HEADS):
            st_s[hd] = s0_ref[hd].T

    r = lax.broadcasted_iota(I32, (tt, tt), 0)
    cc = lax.broadcasted_iota(I32, (tt, tt), 1)
    tri = jnp.where(((r // c) == (cc // c)) & (cc <= r), 1.0, 0.0).astype(BF16)
    for hd in range(HG_HEADS):
        hl = slice(hd * LANE, (hd + 1) * LANE)
        lb = lb_ref[:, hl]
        z = f_ref[:, hl]
        lf = jnp.log(jnp.maximum(lb, LB_FLOOR) + (1.0 - lb) * _sigmoid(z))
        k_s[hd] = (1.0 - lb) * _sigmoid(-z)
        hq = q_ref[:, hl]
        q_s[hd] = hq * _sigmoid(hq)
        v_s[hd] = i_ref[:, hl]
        hi = lf.astype(BF16)
        r1 = lf - hi.astype(F32)
        mid = r1.astype(BF16)
        lo = (r1 - mid.astype(F32)).astype(BF16)
        b_s[hd] = (jnp.dot(tri, hi, preferred_element_type=F32) + jnp.dot(tri, mid, preferred_element_type=F32)
                   + jnp.dot(tri, lo, preferred_element_type=F32))

    row8 = lax.broadcasted_iota(I32, (SUBLANE, LANE), 0)
    nsub = c // SUBLANE

    def block(j, carry):
        base = pl.multiple_of(j * c, c)
        for hd in range(HG_HEADS):
            hl = slice(hd * LANE, (hd + 1) * LANE)
            bj = b_s[hd, pl.ds(base, c), :]
            qj = q_s[hd, pl.ds(base, c), :]
            kj = k_s[hd, pl.ds(base, c), :]
            vj = v_s[hd, pl.ds(base, c), :]
            blast = b_s[hd, pl.ds(base + c - 1, 1), :]
            oc = [jnp.zeros((SUBLANE, LANE), F32) for _ in range(nsub)]
            for s in range(c):
                bs = b_s[hd, pl.ds(base + s, 1), :]
                ks = k_s[hd, pl.ds(base + s, 1), :]
                vs = v_s[hd, pl.ds(base + s, 1), :]
                for u in range(s // SUBLANE, nsub):
                    d = bj[u * SUBLANE:(u + 1) * SUBLANE] - bs
                    if u == s // SUBLANE:
                        d = jnp.where(row8 >= (s % SUBLANE), d, NEG_INF)
                    a = jnp.sum(qj[u * SUBLANE:(u + 1) * SUBLANE] * jnp.exp(d) * ks, axis=-1, keepdims=True)
                    oc[u] = oc[u] + a * vs
            o_intra = jnp.concatenate(oc, axis=0)
            st = st_s[hd]
            qe = qj * jnp.exp(bj)
            o_inter = lax.dot_general(qe.astype(BF16), st.astype(BF16), NT_DIMS, preferred_element_type=F32)
            ke = kj * jnp.exp(blast - bj)
            upd = jnp.dot(vj.T.astype(BF16), ke.astype(BF16), preferred_element_type=F32)
            st_s[hd] = jnp.exp(blast) * st + upd
            o_s[pl.ds(base, c), hl] = o_intra + o_inter
        return carry

    lax.fori_loop(0, tt // c, block, 0)

    for hd in range(HG_HEADS):
        hl = slice(hd * LANE, (hd + 1) * LANE)
        o = o_s[:, hl]
        o = o * lax.rsqrt(jnp.mean(o * o, axis=-1, keepdims=True) + RMS_EPS)
        o_ref[:, hl] = o * ng_ref[:, hl] * _sigmoid(g_ref[:, hl])

    @pl.when(t == pl.num_programs(1) - 1)
    def _():
        for hd in range(HG_HEADS):
            s_ref[hd] = st_s[hd].T


def _hgrn(h, lb, ng, s0, row0, nb, t):
    assert t % HG_BLOCK == 0
    tt = _tile(t, 256, HG_BLOCK)
    nt = t // tt
    rb0 = row0 // tt

    def hspec(col):
        return pl.BlockSpec((tt, HG_WIDTH), lambda b, i: (rb0 + b * nt + i, col // HG_WIDTH))

    vspec = pl.BlockSpec((1, HG_WIDTH), lambda b, i: (0, 0))
    sspec = pl.BlockSpec((None, HG_HEADS, HG_DK, HG_DV), lambda b, i: (b, 0, 0, 0))
    return pl.pallas_call(
        functools.partial(_hgrn_kernel, tt=tt),
        grid=(nb, nt),
        in_specs=[hspec(C_HQ), hspec(C_HF), hspec(C_HI), hspec(C_HG), vspec, vspec, sspec],
        out_specs=[pl.BlockSpec((tt, HG_WIDTH), lambda b, i: (b * nt + i, 0)), sspec],
        out_shape=[jax.ShapeDtypeStruct((nb * t, HG_WIDTH), F32),
                   jax.ShapeDtypeStruct((nb, HG_HEADS, HG_DK, HG_DV), F32)],
        scratch_shapes=[pltpu.VMEM((HG_HEADS, HG_DV, HG_DK), F32)] + [pltpu.VMEM((HG_HEADS, tt, LANE), F32)] * 4
        + [pltpu.VMEM((tt, HG_WIDTH), F32)],
        compiler_params=_cparams(("parallel", "arbitrary")),
        name="hgrn2",
    )(h, h, h, h, lb, ng, s0)


def _dsa_kernel(q_ref, qi_ref, w_ref, k_ref, v_ref, ki_ref, o_ref, kb_s, vt_s, kib_s, sc_s, qh_s, lg_s,
                *, past, nkeys, topk, kc, qblk, jbits):
    qb = pl.program_id(1)
    kf = float(topk)
    acc_rows = 4 * SUBLANE

    @pl.when(qb == 0)
    def _():
        kb_s[...] = k_ref[...].astype(BF16)
        kib_s[...] = ki_ref[...].astype(BF16)
        for c in range(sc_s.shape[0]):
            vt_s[c] = v_ref[c * kc:(c + 1) * kc, :].T.astype(BF16)

    qcol = lax.broadcasted_iota(I32, (1, qblk), 1)
    n_adm = jnp.minimum(((past + qb * qblk + qcol) // CHUNK + 1) * CHUNK, nkeys)
    n_adm_max = jnp.minimum(((past + (qb + 1) * qblk - 1) // CHUNK + 1) * CHUNK, nkeys)
    nkc = (n_adm_max + kc - 1) // kc
    qit = qi_ref[...].T
    wt = (w_ref[...] * (IDX_HEADS ** -0.5)).T
    qip = [jnp.concatenate([qit[h * IDX_DIM:(h + 1) * IDX_DIM] for h in (2 * j, 2 * j + 1)], axis=1).astype(BF16)
           for j in range(IDX_HEADS // 2)]
    wh = [wt[IDX_DIM + h:IDX_DIM + h + 1] for h in range(IDX_HEADS)]
    kpos = lax.broadcasted_iota(I32, (kc, qblk), 0)

    def score_chunk(c, carry):
        mn, mx = carry
        base = pl.multiple_of(c * kc, kc)
        kic = kib_s[pl.ds(base, kc), :]
        sc = jnp.zeros((kc, qblk), F32)
        for j in range(IDX_HEADS // 2):
            rel = jnp.dot(kic, qip[j], preferred_element_type=F32)
            sc = (sc + wh[2 * j] * jnp.maximum(rel[:, :qblk], 0.0)
                  + wh[2 * j + 1] * jnp.maximum(rel[:, qblk:], 0.0))
        adm = (base + kpos) < n_adm
        sc_s[c] = jnp.where(adm, sc, NEG_INF)
        mn = jnp.minimum(mn, jnp.min(jnp.where(adm, sc, jnp.inf), axis=0, keepdims=True))
        mx = jnp.maximum(mx, jnp.max(jnp.where(adm, sc, -jnp.inf), axis=0, keepdims=True))
        return mn, mx

    mn, mx = lax.fori_loop(0, nkc, score_chunk,
                           (jnp.full((1, qblk), jnp.inf, F32), jnp.full((1, qblk), -jnp.inf, F32)))

    def count(pred):
        def body(c, acc):
            hit = jnp.where(pred(c, sc_s[c]), 1.0, 0.0)
            for i in range(kc // acc_rows):
                acc = acc + hit[i * acc_rows:(i + 1) * acc_rows]
            return acc
        acc = lax.fori_loop(0, nkc, body, jnp.zeros((acc_rows, qblk), F32))
        return jnp.sum(acc, axis=0, keepdims=True)

    cl0 = jnp.where(n_adm <= topk, kf, n_adm.astype(F32))
    hi0 = mx + (jnp.abs(mx) + jnp.abs(mn) + 1e-30) * 1e-6

    def mid_of(lo, hi):
        return lo + (hi - lo) * 0.5

    def bis_cond(cy):
        lo, hi, cl, ch, it = cy
        mid = mid_of(lo, hi)
        active = (cl != kf) & (mid > lo) & (mid < hi)
        return jnp.logical_and(it < 2200, jnp.max(jnp.where(active, 1.0, 0.0)) > 0.0)

    def bis_pass(lo, hi, cl, ch, interpolate):
        mid = mid_of(lo, hi)
        if interpolate:
            guess = lo + (hi - lo) * ((cl - kf) / (cl - ch))
            mid = jnp.where((guess > lo) & (guess < hi), guess, mid)
        cnt = count(lambda c, s: s >= mid)
        inside = (mid > lo) & (mid < hi)
        up = inside & (cnt >= kf)
        dn = inside & (cnt < kf)
        return jnp.where(up, mid, lo), jnp.where(dn, mid, hi), jnp.where(up, cnt, cl), jnp.where(dn, cnt, ch)

    def bis_body(cy):
        lo, hi, cl, ch, it = cy
        for i in range(BIS_PASSES_PER_CHECK):
            lo, hi, cl, ch = bis_pass(lo, hi, cl, ch, interpolate=(i % 2 == 0))
        return lo, hi, cl, ch, it + 1

    lo, hi, cl, ch, _ = lax.while_loop(bis_cond, bis_body, (mn, hi0, cl0, jnp.zeros((1, qblk), F32), jnp.int32(0)))

    need = kf - ch

    def tie_limit():
        def bit(i, j):
            jt = j + lax.shift_left(jnp.int32(1), jnp.asarray(jbits - 1 - i, I32))
            cnt = count(lambda c, s: (s >= lo) & (s < hi) & ((c * kc + kpos) < jt))
            return jnp.where(cnt <= need, jt, j)
        return lax.fori_loop(0, jbits, bit, jnp.zeros((1, qblk), I32))

    has_tie = jnp.max(jnp.where(cl > kf, 1.0, 0.0)) > 0.0
    jl = lax.cond(has_tie, tie_limit, lambda: jnp.full((1, qblk), 2 ** 30, I32))

    def mask_chunk(c, carry):
        s = sc_s[c]
        sel = ((s >= hi) | ((s >= lo) & ((c * kc + kpos) < jl))) & (s > 0.5 * NEG_INF)
        sc_s[c] = jnp.where(sel, 0.0, NEG_INF)
        return carry

    lax.fori_loop(0, nkc, mask_chunk, 0)

    qh_s[...] = (q_ref[...] * (DSA_HEAD_DIM ** -0.5)).T.astype(BF16)

    def logit_chunk(c, ms):
        base = pl.multiple_of(c * kc, kc)
        bias = sc_s[c]
        kch = kb_s[pl.ds(base, kc), :]
        out = []
        for h in range(DSA_HEADS):
            n = h // DSA_GROUP
            lg = jnp.dot(kch[:, n * DSA_HEAD_DIM:(n + 1) * DSA_HEAD_DIM],
                         qh_s[h * DSA_HEAD_DIM:(h + 1) * DSA_HEAD_DIM, :], preferred_element_type=F32) + bias
            lg_s[c, h] = lg
            out.append(jnp.maximum(ms[h], jnp.max(lg, axis=0, keepdims=True)))
        return tuple(out)

    ms = lax.fori_loop(0, nkc, logit_chunk, tuple(jnp.full((1, qblk), NEG_INF, F32) for _ in range(DSA_HEADS)))

    def value_chunk(c, carry):
        ls, accs = carry
        ls_new, accs_new = [], []
        for h in range(DSA_HEADS):
            n = h // DSA_GROUP
            p = jnp.exp(lg_s[c, h] - ms[h])
            ls_new.append(ls[h] + jnp.sum(p, axis=0, keepdims=True))
            accs_new.append(accs[h] + jnp.dot(vt_s[c, n * DSA_HEAD_DIM:(n + 1) * DSA_HEAD_DIM, :], p.astype(BF16),
                                              preferred_element_type=F32))
        return tuple(ls_new), tuple(accs_new)

    init = (tuple(jnp.zeros((1, qblk), F32) for _ in range(DSA_HEADS)),
            tuple(jnp.zeros((DSA_HEAD_DIM, qblk), F32) for _ in range(DSA_HEADS)))
    ls, accs = lax.fori_loop(0, nkc, value_chunk, init)
    out = jnp.concatenate([accs[h] / ls[h] for h in range(DSA_HEADS)], axis=0)
    o_ref[...] = out.T


def _dsa(q, qi, kiw, k_all, v_all, ki_all, nb, t, past, topk):
    lp = k_all.shape[1]
    kc = DSA_KEY_CHUNK
    qblk = min(t, LANE)
    assert lp % kc == 0 and kc >= topk and t % qblk == 0 and qblk % CHUNK == 0
    nq = t // qblk
    jbits = int(math.ceil(math.log2(lp))) + 1

    def qspec(width):
        return pl.BlockSpec((qblk, width), lambda b, i: (b * nq + i, 0))

    def kspec(width):
        return pl.BlockSpec((None, lp, width), lambda b, i: (b, 0, 0))

    return pl.pallas_call(
        functools.partial(_dsa_kernel, past=past, nkeys=past + t, topk=topk, kc=kc, qblk=qblk, jbits=jbits),
        grid=(nb, nq),
        in_specs=[qspec(DSA_WIDTH), qspec(IDX_HEADS * IDX_DIM), qspec(LANE), kspec(LANE), kspec(LANE), kspec(IDX_DIM)],
        out_specs=qspec(DSA_WIDTH),
        out_shape=jax.ShapeDtypeStruct((nb * t, DSA_WIDTH), F32),
        scratch_shapes=[pltpu.VMEM((lp, LANE), BF16), pltpu.VMEM((lp // kc, LANE, kc), BF16),
                        pltpu.VMEM((lp, IDX_DIM), BF16), pltpu.VMEM((lp // kc, kc, qblk), F32),
                        pltpu.VMEM((DSA_WIDTH, qblk), BF16), pltpu.VMEM((lp // kc, DSA_HEADS, kc, qblk), F32)],
        compiler_params=_cparams(("parallel", "arbitrary")),
        name="dsa",
    )(q, qi, kiw, k_all, v_all, ki_all)


def _mem_kernel(q_ref, mk_ref, mv_ref, o_ref):
    scale = MEM_HEAD_DIM ** -0.5
    for h in range(MEM_HEADS):
        sl = slice(h * MEM_HEAD_DIM, (h + 1) * MEM_HEAD_DIM)
        qh = q_ref[:, sl].astype(BF16)
        kh = mk_ref[:, sl].astype(BF16)
        vh = mv_ref[:, sl].astype(BF16)
        lg = lax.dot_general(qh, kh, NT_DIMS, preferred_element_type=F32) * scale
        p = jnp.exp(lg - jnp.max(lg, axis=-1, keepdims=True))
        o = jnp.dot(p.astype(BF16), vh, preferred_element_type=F32)
        o_ref[:, sl] = o / jnp.sum(p, axis=-1, keepdims=True)


def _mem_attn(h, mk, mv, kcol, vcol, row0, nb, t):
    tq = _tile(t, 256)
    nt = t // tq
    rb0 = row0 // tq
    m = mk.shape[1]
    return pl.pallas_call(
        _mem_kernel,
        grid=(nb, nt),
        in_specs=[pl.BlockSpec((tq, MEM_WIDTH), lambda b, i: (rb0 + b * nt + i, C_MQ // MEM_WIDTH)),
                  pl.BlockSpec((None, m, MEM_WIDTH), lambda b, i: (b, 0, kcol)),
                  pl.BlockSpec((None, m, MEM_WIDTH), lambda b, i: (b, 0, vcol))],
        out_specs=pl.BlockSpec((tq, MEM_WIDTH), lambda b, i: (b * nt + i, 0)),
        out_shape=jax.ShapeDtypeStruct((nb * t, MEM_WIDTH), F32),
        compiler_params=_cparams(("parallel", "parallel")),
        name="mem_attn",
    )(h, mk, mv)


def _merge_kernel(bh_ref, bd_ref, bm_ref, g0_ref, g1_ref, g2_ref, x_ref, wb_ref, wo_ref, lg_ref, lb_ref, o_ref,
                  *, alpha):
    merged = None
    for br_ref, g_ref, n in ((bh_ref, g0_ref, 0), (bd_ref, g1_ref, 1), (bm_ref, g2_ref, 2)):
        proj = jnp.dot(br_ref[...].astype(BF16), wb_ref[n], preferred_element_type=F32)
        term = _sigmoid(g_ref[...]) * proj
        merged = term if merged is None else merged + term
    m = jnp.dot(merged.astype(BF16), wo_ref[...], preferred_element_type=F32)
    o_ref[...] = _layer_norm(alpha * x_ref[...] + m, lg_ref[...], lb_ref[...])


def _merge(o_hg, o_dsa, o_mem, h, x, wb, wo, lng, lnb, alpha):
    n = x.shape[0]
    tm = _tile(n, 256)
    bspec = pl.BlockSpec((tm, 512), lambda i: (i, 0))

    def gspec(j):
        return pl.BlockSpec((tm, D_MODEL), lambda i: (i, C_GATE // D_MODEL + j))

    xspec = pl.BlockSpec((tm, D_MODEL), lambda i: (i, 0))
    vspec = pl.BlockSpec((1, D_MODEL), lambda i: (0, 0))
    return pl.pallas_call(
        functools.partial(_merge_kernel, alpha=alpha),
        grid=(n // tm,),
        in_specs=[bspec, bspec, bspec, gspec(0), gspec(1), gspec(2), xspec,
                  pl.BlockSpec((N_BRANCH, 512, D_MODEL), lambda i: (0, 0, 0)),
                  pl.BlockSpec((D_MODEL, D_MODEL), lambda i: (0, 0)), vspec, vspec],
        out_specs=xspec,
        out_shape=jax.ShapeDtypeStruct((n, D_MODEL), F32),
        compiler_params=_cparams(("parallel",)),
        name="merge",
    )(o_hg, o_dsa, o_mem, h, h, h, x, wb, wo, lng, lnb)


def _swiglu_partial(xb, wg_ref, wu_ref, wd_ref):
    g = jnp.dot(xb, wg_ref[...], preferred_element_type=F32)
    u = jnp.dot(xb, wu_ref[...], preferred_element_type=F32)
    a = (g * _sigmoid(g)) * u
    return jnp.dot(a.astype(BF16), wd_ref[...], preferred_element_type=F32)


def _ffn_kernel(x_ref, wg_ref, wu_ref, wd_ref, lg_ref, lb_ref, o_ref, xb_s, acc_s, *, alpha):
    k = pl.program_id(1)

    @pl.when(k == 0)
    def _():
        xb_s[...] = x_ref[...].astype(BF16)
        acc_s[...] = jnp.zeros_like(acc_s)

    acc_s[...] += _swiglu_partial(xb_s[...], wg_ref, wu_ref, wd_ref)

    @pl.when(k == pl.num_programs(1) - 1)
    def _():
        o_ref[...] = _layer_norm(alpha * x_ref[...] + acc_s[...], lg_ref[...], lb_ref[...])


def _ffn_dense(x, w_gu, w_down, lng, lnb, alpha):
    n = x.shape[0]
    f = w_down.shape[0]
    tm = _tile(n, 512)
    fc = _tile(f, 1536, LANE)
    nk = f // fc
    xspec = pl.BlockSpec((tm, D_MODEL), lambda i, k: (i, 0))
    vspec = pl.BlockSpec((1, D_MODEL), lambda i, k: (0, 0))
    return pl.pallas_call(
        functools.partial(_ffn_kernel, alpha=alpha),
        grid=(n // tm, nk),
        in_specs=[xspec,
                  pl.BlockSpec((D_MODEL, fc), lambda i, k: (0, k)),
                  pl.BlockSpec((D_MODEL, fc), lambda i, k: (0, nk + k)),
                  pl.BlockSpec((fc, D_MODEL), lambda i, k: (k, 0)), vspec, vspec],
        out_specs=xspec,
        out_shape=jax.ShapeDtypeStruct((n, D_MODEL), F32),
        scratch_shapes=[pltpu.VMEM((tm, D_MODEL), BF16), pltpu.VMEM((tm, D_MODEL), F32)],
        compiler_params=_cparams(("parallel", "arbitrary")),
        name="ffn_dense",
    )(x, w_gu, w_gu, w_down, lng, lnb)


def _router_kernel(x_ref, w_ref, e_ref, g_ref):
    x = x_ref[...]
    w = w_ref[...]
    xh = x.astype(BF16)
    xl = (x - xh.astype(F32)).astype(BF16)
    wh = w.astype(BF16)
    wl = (w - wh.astype(F32)).astype(BF16)
    lg = (jnp.dot(xh, wh, preferred_element_type=F32) + jnp.dot(xh, wl, preferred_element_type=F32)
          + jnp.dot(xl, wh, preferred_element_type=F32))
    lane = lax.broadcasted_iota(I32, lg.shape, 1).astype(F32)
    lg = jnp.where(lane < N_EXPERTS, lg, -jnp.inf)
    m1 = jnp.max(lg, axis=-1, keepdims=True)
    i1 = jnp.min(jnp.where(lg == m1, lane, float(LANE)), axis=-1, keepdims=True)
    lg2 = jnp.where(lane == i1, -jnp.inf, lg)
    m2 = jnp.max(lg2, axis=-1, keepdims=True)
    i2 = jnp.min(jnp.where(lg2 == m2, lane, float(LANE)), axis=-1, keepdims=True)
    e = jnp.exp(m2 - m1)
    g1 = 1.0 / (1.0 + e)
    g2 = e / (1.0 + e)
    e_ref[...] = jnp.where(lane == 0.0, i1, jnp.where(lane == 1.0, i2, 0.0)).astype(I32)
    g_ref[...] = jnp.where(lane == 0.0, g1, jnp.where(lane == 1.0, g2, 0.0))


def _router(x, w_router):
    n = x.shape[0]
    tm = _tile(n, 512)
    wp = jnp.pad(w_router, ((0, 0), (0, LANE - N_EXPERTS)))
    ospec = pl.BlockSpec((tm, LANE), lambda i: (i, 0))
    return pl.pallas_call(
        _router_kernel,
        grid=(n // tm,),
        in_specs=[pl.BlockSpec((tm, D_MODEL), lambda i: (i, 0)), pl.BlockSpec((D_MODEL, LANE), lambda i: (0, 0))],
        out_specs=[ospec, ospec],
        out_shape=[jax.ShapeDtypeStruct((n, LANE), I32), jax.ShapeDtypeStruct((n, LANE), F32)],
        compiler_params=_cparams(("parallel",)),
        name="router",
    )(x, wp)


def _row_copy(src, dst, s, d, sem):
    return pltpu.make_async_copy(src.at[pl.ds(s, 1)], dst.at[pl.ds(d, 1)], sem)


def _dispatch_kernel(pos_ref, x_ref, xs_in, xs_hbm, sem, *, tb):
    del xs_in
    i = pl.program_id(0)

    def start(r, c):
        t = i * tb + r
        for j in range(TOP_K):
            _row_copy(x_ref, xs_hbm, r, pos_ref[TOP_K * t + j], sem).start()
        return c

    lax.fori_loop(0, tb, start, 0)

    def wait(r, c):
        for j in range(TOP_K):
            _row_copy(x_ref, xs_hbm, 0, 0, sem).wait()
        return c

    lax.fori_loop(0, tb, wait, 0)


def _dispatch(pos, x, n_rows):
    n = x.shape[0]
    tb = _tile(n, 256)
    xs0 = jnp.zeros((n_rows, D_MODEL), F32)
    return pl.pallas_call(
        functools.partial(_dispatch_kernel, tb=tb),
        grid_spec=pltpu.PrefetchScalarGridSpec(
            num_scalar_prefetch=1, grid=(n // tb,),
            in_specs=[pl.BlockSpec((tb, D_MODEL), lambda i, p: (i, 0)), pl.BlockSpec(memory_space=pl.ANY)],
            out_specs=pl.BlockSpec(memory_space=pl.ANY),
            scratch_shapes=[pltpu.SemaphoreType.DMA(())]),
        out_shape=jax.ShapeDtypeStruct((n_rows, D_MODEL), F32),
        input_output_aliases={2: 0},
        compiler_params=pltpu.CompilerParams(dimension_semantics=("arbitrary",), has_side_effects=True),
        name="moe_dispatch",
    )(pos, x, xs0)


def _expert_kernel(te_ref, nu_ref, x_ref, wg_ref, wu_ref, wd_ref, o_ref, xb_s, acc_s):
    del te_ref
    i = pl.program_id(0)
    k = pl.program_id(1)
    used = i < nu_ref[0]

    @pl.when(jnp.logical_and(used, k == 0))
    def _():
        xb_s[...] = x_ref[...].astype(BF16)
        acc_s[...] = jnp.zeros_like(acc_s)

    @pl.when(used)
    def _():
        acc_s[...] += _swiglu_partial(xb_s[...], wg_ref, wu_ref, wd_ref)

    @pl.when(k == pl.num_programs(1) - 1)
    def _():
        o_ref[...] = jnp.where(used, acc_s[...], 0.0)


def _experts(tile_expert, n_used, xs, w_gu, w_down, tm):
    n_rows = xs.shape[0]
    f = w_down.shape[1]
    fc = _tile(f, 1024, LANE)
    nk = f // fc
    xspec = pl.BlockSpec((tm, D_MODEL), lambda i, k, te, nu: (i, 0))
    return pl.pallas_call(
        _expert_kernel,
        grid_spec=pltpu.PrefetchScalarGridSpec(
            num_scalar_prefetch=2, grid=(n_rows // tm, nk),
            in_specs=[xspec,
                      pl.BlockSpec((None, D_MODEL, fc), lambda i, k, te, nu: (te[i], 0, k)),
                      pl.BlockSpec((None, D_MODEL, fc), lambda i, k, te, nu: (te[i], 0, nk + k)),
                      pl.BlockSpec((None, fc, D_MODEL), lambda i, k, te, nu: (te[i], k, 0))],
            out_specs=xspec,
            scratch_shapes=[pltpu.VMEM((tm, D_MODEL), BF16), pltpu.VMEM((tm, D_MODEL), F32)]),
        out_shape=jax.ShapeDtypeStruct((n_rows, D_MODEL), F32),
        compiler_params=_cparams(("arbitrary", "arbitrary")),
        name="moe_experts",
    )(tile_expert, n_used, xs, w_gu, w_gu, w_down)


def _combine_kernel(pos_ref, ys_hbm, g_ref, x_ref, lg_ref, lb_ref, o_ref, buf, sem, *, tb, alpha):
    i = pl.program_id(0)

    def start(r, c):
        t = i * tb + r
        for j in range(TOP_K):
            _row_copy(ys_hbm, buf.at[j], pos_ref[TOP_K * t + j], r, sem.at[j]).start()
        return c

    lax.fori_loop(0, tb, start, 0)

    def wait(r, c):
        for j in range(TOP_K):
            _row_copy(ys_hbm, buf.at[j], 0, 0, sem.at[j]).wait()
        return c

    lax.fori_loop(0, tb, wait, 0)
    g = g_ref[...]
    f = g[:, 0:1] * buf[0] + g[:, 1:2] * buf[1]
    o_ref[...] = _layer_norm(alpha * x_ref[...] + f, lg_ref[...], lb_ref[...])


def _combine(pos, ys, gates, x, lng, lnb, alpha):
    n = x.shape[0]
    tb = _tile(n, 256)
    xspec = pl.BlockSpec((tb, D_MODEL), lambda i, p: (i, 0))
    vspec = pl.BlockSpec((1, D_MODEL), lambda i, p: (0, 0))
    return pl.pallas_call(
        functools.partial(_combine_kernel, tb=tb, alpha=alpha),
        grid_spec=pltpu.PrefetchScalarGridSpec(
            num_scalar_prefetch=1, grid=(n // tb,),
            in_specs=[pl.BlockSpec(memory_space=pl.ANY), pl.BlockSpec((tb, LANE), lambda i, p: (i, 0)),
                      xspec, vspec, vspec],
            out_specs=xspec,
            scratch_shapes=[pltpu.VMEM((TOP_K, tb, D_MODEL), F32), pltpu.SemaphoreType.DMA((TOP_K,))]),
        out_shape=jax.ShapeDtypeStruct((n, D_MODEL), F32),
        compiler_params=_cparams(("arbitrary",)),
        name="moe_combine",
    )(pos, ys, gates, x, lng, lnb)


def _moe(x, w_router, w_gu, w_down, lng, lnb, alpha):
    n = x.shape[0]
    tm = _tile(n, 512)
    eidx, gates = _router(x, w_router)
    e2 = eidx[:, :TOP_K].reshape(-1)
    onehot = (e2[:, None] == jnp.arange(N_EXPERTS, dtype=I32)[None, :]).astype(I32)
    csum = jnp.cumsum(onehot, axis=0)
    rank = jnp.take_along_axis(csum, e2[:, None], axis=1)[:, 0] - 1
    counts = csum[-1]
    padded = ((counts + tm - 1) // tm) * tm
    ends = jnp.cumsum(padded)
    pos = ((ends - padded)[e2] + rank).astype(I32)
    n_rows = TOP_K * n + N_EXPERTS * tm
    n_tiles = n_rows // tm
    tile_start = jnp.arange(n_tiles, dtype=I32) * tm
    tile_expert = jnp.minimum(jnp.sum((tile_start[:, None] >= ends[None, :]).astype(I32), axis=1),
                              N_EXPERTS - 1).astype(I32)
    n_used = (ends[-1:] // tm).astype(I32)
    xs = _dispatch(pos, x, n_rows)
    ys = _experts(tile_expert, n_used, xs, w_gu, w_down, tm)
    return _combine(pos, ys, gates, x, lng, lnb, alpha)


def _pack_w_in(w):
    d = w.shape[0]
    parts, start = [], 0
    for n in IN_SIZES:
        parts.append(w[:, start:start + n])
        start += n
    hq, hf, hi, hg, dq, dk, dv, iq, ik, iw, mq, gates = parts
    ikw = jnp.concatenate([ik, iw, jnp.zeros((d, LANE - IDX_DIM - IDX_HEADS), w.dtype)], axis=1)
    tail = jnp.zeros((d, H_COLS - (C_IK + LANE)), w.dtype)
    return jnp.concatenate([hq, hf, hi, hg, gates, dq, mq, iq, dk, dv, ikw, tail], axis=1).astype(BF16)


def kernel(x_prompt, x_sample, cache_dsa_k, cache_dsa_v, cache_idx_k, state_hgrn, cache_mem_k, cache_mem_v,
           mem_prompt, w_in, hgrn_lb_logits, hgrn_norm_g, w_branch, w_out, w_mem_kv, ln_g, ln_b, ffn_w_gate_up,
           ffn_w_down, moe_router, moe_w_gate_up, moe_w_down):
    depth = w_in.shape[0]
    alpha = (2 * depth) ** 0.25
    bp, tp, d = x_prompt.shape
    bs, ts, _ = x_sample.shape
    past = cache_dsa_k.shape[2]
    m = mem_prompt.shape[1]
    np_rows, ns_rows = bp * tp, bs * ts

    sm = jax.nn.softmax(hgrn_lb_logits.astype(F32), axis=0)
    lbs = jnp.cumsum(sm, axis=0) - sm[0:1]

    x = jnp.concatenate([x_prompt.reshape(np_rows, d), x_sample.reshape(ns_rows, d)], axis=0)
    n = x.shape[0]
    tables_p = _rope_tables(jnp.arange(tp, dtype=I32))
    tables_s = _rope_tables(past + jnp.arange(ts, dtype=I32))
    mem_flat = mem_prompt.reshape(bp * m, d)
    zeros_state = jnp.zeros((bp, HG_HEADS, HG_DK, HG_DV), F32)
    kc = DSA_KEY_CHUNK
    lp_s = ((past + ts + kc - 1) // kc) * kc

    outs = {k: [] for k in ("pk", "pv", "pki", "ph", "pmk", "pmv", "sk", "sv", "ski", "sh")}
    for l in range(depth):
        h = _matmul(x, _pack_w_in(w_in[l]), _tile(n, 512), IN_TN, "in_proj")
        mem_kv = _matmul(mem_flat, w_mem_kv[l].astype(BF16), _tile(bp * m, 512), 512, "mem_kv").reshape(bp, m, 2 * MEM_WIDTH)
        lb = lbs[l].reshape(1, HG_WIDTH)
        ng = hgrn_norm_g[l].reshape(1, HG_WIDTH).astype(F32)

        qp, kp, qip, kip = _rope(h, tables_p, 0, bp, tp)
        vp = h[:np_rows, C_DV:C_DV + LANE]
        kip32 = kip[:, :IDX_DIM]
        ohg_p, sh_p = _hgrn(h, lb, ng, zeros_state, 0, bp, tp)
        odsa_p = _dsa(qp, qip, kip, kp.reshape(bp, tp, LANE), vp.reshape(bp, tp, LANE), kip32.reshape(bp, tp, IDX_DIM),
                      bp, tp, 0, min(TOPK_MAX, tp // 4))
        omem_p = _mem_attn(h, mem_kv, mem_kv, 0, 1, 0, bp, tp)

        qs, ks, qis, kis = _rope(h, tables_s, np_rows, bs, ts)
        vs = h[np_rows:, C_DV:C_DV + LANE]
        kis32 = kis[:, :IDX_DIM]
        npad = lp_s - past - ts
        k_all = jnp.concatenate([cache_dsa_k[l].reshape(bs, past, LANE), ks.reshape(bs, ts, LANE),
                                 jnp.zeros((bs, npad, LANE), F32)], axis=1)
        v_all = jnp.concatenate([cache_dsa_v[l].reshape(bs, past, LANE), vs.reshape(bs, ts, LANE),
                                 jnp.zeros((bs, npad, LANE), F32)], axis=1)
        ki_all = jnp.concatenate([cache_idx_k[l], kis32.reshape(bs, ts, IDX_DIM),
                                  jnp.zeros((bs, npad, IDX_DIM), F32)], axis=1)
        ohg_s, sh_s = _hgrn(h, lb, ng, state_hgrn[l], np_rows, bs, ts)
        odsa_s = _dsa(qs, qis, kis, k_all, v_all, ki_all, bs, ts, past, min(TOPK_MAX, (past + ts) // 4))
        omem_s = _mem_attn(h, cache_mem_k[l].reshape(bs, m, MEM_WIDTH), cache_mem_v[l].reshape(bs, m, MEM_WIDTH),
                           0, 0, np_rows, bs, ts)

        o_hg = jnp.concatenate([ohg_p, ohg_s], axis=0)
        o_dsa = jnp.concatenate([odsa_p, odsa_s], axis=0)
        o_mem = jnp.concatenate([omem_p, omem_s], axis=0)
        x = _merge(o_hg, o_dsa, o_mem, h, x, w_branch[l].astype(BF16), w_out[l].astype(BF16),
                   ln_g[l, 0].reshape(1, d), ln_b[l, 0].reshape(1, d), alpha)
        if l % 2 == 0:
            x = _ffn_dense(x, ffn_w_gate_up[l // 2].astype(BF16), ffn_w_down[l // 2].astype(BF16),
                           ln_g[l, 1].reshape(1, d), ln_b[l, 1].reshape(1, d), alpha)
        else:
            x = _moe(x, moe_router[l // 2], moe_w_gate_up[l // 2].astype(BF16), moe_w_down[l // 2].astype(BF16),
                     ln_g[l, 1].reshape(1, d), ln_b[l, 1].reshape(1, d), alpha)

        outs["pk"].append(kp.reshape(bp, tp, DSA_KV_HEADS, DSA_HEAD_DIM))
        outs["pv"].append(vp.reshape(bp, tp, DSA_KV_HEADS, DSA_HEAD_DIM))
        outs["pki"].append(kip32.reshape(bp, tp, IDX_DIM))
        outs["ph"].append(sh_p)
        outs["pmk"].append(mem_kv[..., :MEM_WIDTH].reshape(bp, m, MEM_HEADS, MEM_HEAD_DIM))
        outs["pmv"].append(mem_kv[..., MEM_WIDTH:].reshape(bp, m, MEM_HEADS, MEM_HEAD_DIM))
        outs["sk"].append(ks.reshape(bs, ts, DSA_KV_HEADS, DSA_HEAD_DIM))
        outs["sv"].append(vs.reshape(bs, ts, DSA_KV_HEADS, DSA_HEAD_DIM))
        outs["ski"].append(kis32.reshape(bs, ts, IDX_DIM))
        outs["sh"].append(sh_s)

    st = {k: jnp.stack(v) for k, v in outs.items()}
    y_prompt = x[:np_rows].reshape(bp, tp, d)
    y_sample = x[np_rows:].reshape(bs, ts, d)
    return (y_prompt, y_sample, st["pk"], st["pv"], st["pki"], st["ph"], st["pmk"], st["pmv"],
            st["sk"], st["sv"], st["ski"], st["sh"])
```

```python
import functools
import math

import jax
import jax.numpy as jnp
from jax import lax
from jax.experimental import pallas as pl
from jax.experimental.pallas import tpu as pltpu

F32, BF16, I32 = jnp.float32, jnp.bfloat16, jnp.int32

D_MODEL = 1024
CHUNK = 64
HG_HEADS, HG_DK, HG_DV, HG_BLOCK = 4, 128, 128, 32
HG_WIDTH = HG_HEADS * HG_DV
LB_FLOOR = 1e-30
DSA_HEADS, DSA_KV_HEADS, DSA_HEAD_DIM = 8, 2, 64
DSA_GROUP = DSA_HEADS // DSA_KV_HEADS
DSA_WIDTH = DSA_HEADS * DSA_HEAD_DIM
IDX_HEADS, IDX_DIM = 8, 32
TOPK_MAX = 256
Q_BLOCK = 64
MEM_HEADS, MEM_HEAD_DIM = 4, 128
MEM_WIDTH = MEM_HEADS * MEM_HEAD_DIM
N_BRANCH, BRANCH_WIDTH = 3, 512
N_EXPERTS, TOP_K = 8, 2
ROPE_THETA = 10000.0
LN_EPS = 1e-5
RMS_EPS = 1e-6
NEG_INF = -1e30
IN_SIZES = (HG_HEADS * HG_DK, HG_HEADS * HG_DK, HG_HEADS * HG_DV, HG_WIDTH,
            DSA_WIDTH, DSA_KV_HEADS * DSA_HEAD_DIM, DSA_KV_HEADS * DSA_HEAD_DIM,
            IDX_HEADS * IDX_DIM, IDX_DIM, IDX_HEADS, MEM_WIDTH, N_BRANCH * D_MODEL)

LANE = 128
SUBLANE = 8
VMEM_LIMIT = 48 * 1024 * 1024

C_HQ, C_HF, C_HI, C_HG = 0, 512, 1024, 1536
C_GATE = 2048
C_DQ = 5120
C_MQ = 5632
C_IQ = 6144
C_DK = 6400
C_DV = 6528
C_IK = 6656
H_COLS = 6912
IN_TN = 768

DSA_KEY_CHUNK = 256
BIS_PASSES_PER_CHECK = 4

NT_DIMS = (((1,), (1,)), ((), ()))


def _cparams(sem):
    return pltpu.CompilerParams(dimension_semantics=sem, vmem_limit_bytes=VMEM_LIMIT)


def _tile(n, pref, mult=SUBLANE):
    t = min(n, pref)
    while n % t or t % mult:
        t -= 1
    return t


def _sigmoid(x):
    return 1.0 / (1.0 + jnp.exp(-x))


def _layer_norm(v, g, b):
    mu = jnp.mean(v, axis=-1, keepdims=True)
    d = v - mu
    var = jnp.mean(d * d, axis=-1, keepdims=True)
    return d * lax.rsqrt(var + LN_EPS) * g + b


def _mm_kernel(x_ref, w_ref, o_ref):
    o_ref[...] = jnp.dot(x_ref[...].astype(BF16), w_ref[...], preferred_element_type=F32)


def _matmul(x, w, tm, tn, name):
    n, k = x.shape
    nc = w.shape[1]
    return pl.pallas_call(
        _mm_kernel,
        grid=(n // tm, nc // tn),
        in_specs=[pl.BlockSpec((tm, k), lambda i, j: (i, 0)),
                  pl.BlockSpec((k, tn), lambda i, j: (0, j))],
        out_specs=pl.BlockSpec((tm, tn), lambda i, j: (i, j)),
        out_shape=jax.ShapeDtypeStruct((n, nc), F32),
        compiler_params=_cparams(("parallel", "arbitrary")),
        name=name,
    )(x, w)


def _rot_half(x, half):
    lane = lax.broadcasted_iota(I32, x.shape, 1)
    lo = (lane % (2 * half)) < half
    return jnp.where(lo, pltpu.roll(x, LANE - half, 1), pltpu.roll(x, half, 1))


def _rope_kernel(dq_ref, dk_ref, iq_ref, ik_ref, c64_ref, s64_ref, c32_ref, s32_ref, cik_ref, sik_ref,
                 q_ref, k_ref, qi_ref, ki_ref):
    c64, s64 = c64_ref[...], s64_ref[...]
    c32, s32 = c32_ref[...], s32_ref[...]
    for j in range(DSA_WIDTH // LANE):
        x = dq_ref[:, j * LANE:(j + 1) * LANE]
        q_ref[:, j * LANE:(j + 1) * LANE] = x * c64 + _rot_half(x, DSA_HEAD_DIM // 2) * s64
    x = dk_ref[...]
    k_ref[...] = x * c64 + _rot_half(x, DSA_HEAD_DIM // 2) * s64
    for j in range(IDX_HEADS * IDX_DIM // LANE):
        x = iq_ref[:, j * LANE:(j + 1) * LANE]
        qi_ref[:, j * LANE:(j + 1) * LANE] = x * c32 + _rot_half(x, IDX_DIM // 2) * s32
    x = ik_ref[...]
    ki_ref[...] = x * cik_ref[...] + _rot_half(x, IDX_DIM // 2) * sik_ref[...]


def _rope_tables(pos):
    lane = jnp.arange(LANE)
    p = pos.astype(F32)[:, None]
    h64 = DSA_HEAD_DIM // 2
    inv64 = ROPE_THETA ** (-jnp.arange(h64, dtype=F32) / h64)
    a64 = p * inv64[lane % h64][None, :]
    c64 = jnp.cos(a64)
    s64 = jnp.sin(a64) * jnp.where((lane % DSA_HEAD_DIM) < h64, -1.0, 1.0)[None, :]
    h32 = IDX_DIM // 2
    inv32 = ROPE_THETA ** (-jnp.arange(h32, dtype=F32) / h32)
    a32 = p * inv32[lane % h32][None, :]
    c32 = jnp.cos(a32)
    s32 = jnp.sin(a32) * jnp.where((lane % IDX_DIM) < h32, -1.0, 1.0)[None, :]
    live = (lane < IDX_DIM)[None, :]
    cik = jnp.where(live, c32, 1.0)
    sik = jnp.where(live, s32, 0.0)
    return tuple(t.astype(F32) for t in (c64, s64, c32, s32, cik, sik))


def _rope(h, tables, row0, nb, t):
    tr = _tile(t, 256)
    nt = t // tr
    rb0 = row0 // tr
    rows = nb * t

    def hspec(width, col):
        return pl.BlockSpec((tr, width), lambda i: (rb0 + i, col // width))

    tspec = pl.BlockSpec((tr, LANE), lambda i: (i % nt, 0))

    def ospec(width):
        return pl.BlockSpec((tr, width), lambda i: (i, 0))

    return pl.pallas_call(
        _rope_kernel,
        grid=(rows // tr,),
        in_specs=[hspec(DSA_WIDTH, C_DQ), hspec(LANE, C_DK), hspec(IDX_HEADS * IDX_DIM, C_IQ), hspec(LANE, C_IK)]
        + [tspec] * 6,
        out_specs=[ospec(DSA_WIDTH), ospec(LANE), ospec(IDX_HEADS * IDX_DIM), ospec(LANE)],
        out_shape=[jax.ShapeDtypeStruct((rows, DSA_WIDTH), F32), jax.ShapeDtypeStruct((rows, LANE), F32),
                   jax.ShapeDtypeStruct((rows, IDX_HEADS * IDX_DIM), F32), jax.ShapeDtypeStruct((rows, LANE), F32)],
        compiler_params=_cparams(("parallel",)),
        name="rope",
    )(h, h, h, h, *tables)


def _hgrn_kernel(q_ref, f_ref, i_ref, g_ref, lb_ref, ng_ref, s0_ref, o_ref, s_ref,
                 st_s, b_s, k_s, v_s, q_s, o_s, *, tt):
    t = pl.program_id(1)
    c = HG_BLOCK

    @pl.when(t == 0)
    def _():
        for hd in range(HG_HEADS):
            st_s[hd] = s0_ref[hd].T

    r = lax.broadcasted_iota(I32, (tt, tt), 0)
    cc = lax.broadcasted_iota(I32, (tt, tt), 1)
    tri = jnp.where(((r // c) == (cc // c)) & (cc <= r), 1.0, 0.0).astype(BF16)
    for hd in range(HG_HEADS):
        hl = slice(hd * LANE, (hd + 1) * LANE)
        lb = lb_ref[:, hl]
        z = f_ref[:, hl]
        lf = jnp.log(jnp.maximum(lb, LB_FLOOR) + (1.0 - lb) * _sigmoid(z))
        k_s[hd] = (1.0 - lb) * _sigmoid(-z)
        hq = q_ref[:, hl]
        q_s[hd] = hq * _sigmoid(hq)
        v_s[hd] = i_ref[:, hl]
        hi = lf.astype(BF16)
        r1 = lf - hi.astype(F32)
        mid = r1.astype(BF16)
        lo = (r1 - mid.astype(F32)).astype(BF16)
        b_s[hd] = (jnp.dot(tri, hi, preferred_element_type=F32) + jnp.dot(tri, mid, preferred_element_type=F32)
                   + jnp.dot(tri, lo, preferred_element_type=F32))

    row8 = lax.broadcasted_iota(I32, (SUBLANE, LANE), 0)
    nsub = c // SUBLANE

    def block(j, carry):
        base = pl.multiple_of(j * c, c)
        for hd in range(HG_HEADS):
            hl = slice(hd * LANE, (hd + 1) * LANE)
            bj = b_s[hd, pl.ds(base, c), :]
            qj = q_s[hd, pl.ds(base, c), :]
            kj = k_s[hd, pl.ds(base, c), :]
            vj = v_s[hd, pl.ds(base, c), :]
            blast = b_s[hd, pl.ds(base + c - 1, 1), :]
            oc = [jnp.zeros((SUBLANE, LANE), F32) for _ in range(nsub)]
            for s in range(c):
                bs = b_s[hd, pl.ds(base + s, 1), :]
                ks = k_s[hd, pl.ds(base + s, 1), :]
                vs = v_s[hd, pl.ds(base + s, 1), :]
                for u in range(s // SUBLANE, nsub):
                    d = bj[u * SUBLANE:(u + 1) * SUBLANE] - bs
                    if u == s // SUBLANE:
                        d = jnp.where(row8 >= (s % SUBLANE), d, NEG_INF)
                    a = jnp.sum(qj[u * SUBLANE:(u + 1) * SUBLANE] * jnp.exp(d) * ks, axis=-1, keepdims=True)
                    oc[u] = oc[u] + a * vs
            o_intra = jnp.concatenate(oc, axis=0)
            st = st_s[hd]
            qe = qj * jnp.exp(bj)
            o_inter = lax.dot_general(qe.astype(BF16), st.astype(BF16), NT_DIMS, preferred_element_type=F32)
            ke = kj * jnp.exp(blast - bj)
            upd = jnp.dot(vj.T.astype(BF16), ke.astype(BF16), preferred_element_type=F32)
            st_s[hd] = jnp.exp(blast) * st + upd
            o_s[pl.ds(base, c), hl] = o_intra + o_inter
        return carry

    lax.fori_loop(0, tt // c, block, 0)

    for hd in range(HG_HEADS):
        hl = slice(hd * LANE, (hd + 1) * LANE)
        o = o_s[:, hl]
        o = o * lax.rsqrt(jnp.mean(o * o, axis=-1, keepdims=True) + RMS_EPS)
        o_ref[:, hl] = o * ng_ref[:, hl] * _sigmoid(g_ref[:, hl])

    @pl.when(t == pl.num_programs(1) - 1)
    def _():
        for hd in range(HG_HEADS):
            s_ref[hd] = st_s[hd].T


def _hgrn(h, lb, ng, s0, row0, nb, t):
    assert t % HG_BLOCK == 0
    tt = _tile(t, 256, HG_BLOCK)
    nt = t // tt
    rb0 = row0 // tt

    def hspec(col):
        return pl.BlockSpec((tt, HG_WIDTH), lambda b, i: (rb0 + b * nt + i, col // HG_WIDTH))

    vspec = pl.BlockSpec((1, HG_WIDTH), lambda b, i: (0, 0))
    sspec = pl.BlockSpec((None, HG_HEADS, HG_DK, HG_DV), lambda b, i: (b, 0, 0, 0))
    return pl.pallas_call(
        functools.partial(_hgrn_kernel, tt=tt),
        grid=(nb, nt),
        in_specs=[hspec(C_HQ), hspec(C_HF), hspec(C_HI), hspec(C_HG), vspec, vspec, sspec],
        out_specs=[pl.BlockSpec((tt, HG_WIDTH), lambda b, i: (b * nt + i, 0)), sspec],
        out_shape=[jax.ShapeDtypeStruct((nb * t, HG_WIDTH), F32),
                   jax.ShapeDtypeStruct((nb, HG_HEADS, HG_DK, HG_DV), F32)],
        scratch_shapes=[pltpu.VMEM((HG_HEADS, HG_DV, HG_DK), F32)] + [pltpu.VMEM((HG_HEADS, tt, LANE), F32)] * 4
        + [pltpu.VMEM((tt, HG_WIDTH), F32)],
        compiler_params=_cparams(("parallel", "arbitrary")),
        name="hgrn2",
    )(h, h, h, h, lb, ng, s0)


def _dsa_kernel(q_ref, qi_ref, w_ref, *refs, past, nkeys, topk, kc, qblk, jbits):
    n_in = 6 if past else 3
    key_refs, (o_ref, kb_s, vt_s, kib_s, sc_s, qh_s, lg_s) = refs[:n_in], refs[n_in:]
    qb = pl.program_id(1)
    kf = float(topk)
    acc_rows = 4 * SUBLANE
    lp = kb_s.shape[0]
    segs = [(past, nkeys - past) + tuple(key_refs[-3:])]
    if past:
        segs.insert(0, (0, past) + tuple(key_refs[:3]))

    @pl.when(qb == 0)
    def _():
        for s0, n, k_ref, _, ki_ref in segs:
            kb_s[s0:s0 + n, :] = k_ref[...].astype(BF16)
            kib_s[s0:s0 + n, :] = ki_ref[...].astype(BF16)
        if lp > nkeys:
            kb_s[nkeys:, :] = jnp.zeros((lp - nkeys, LANE), BF16)
            kib_s[nkeys:, :] = jnp.zeros((lp - nkeys, IDX_DIM), BF16)
        for c in range(lp // kc):
            pieces = []
            for s0, n, _, v_ref, _ in segs:
                a, b = max(c * kc, s0), min((c + 1) * kc, s0 + n)
                if a < b:
                    pieces.append(v_ref[a - s0:b - s0, :])
            have = sum(p.shape[0] for p in pieces)
            if have < kc:
                pieces.append(jnp.zeros((kc - have, LANE), F32))
            vt_s[c] = (pieces[0] if len(pieces) == 1 else jnp.concatenate(pieces, axis=0)).T.astype(BF16)

    qcol = lax.broadcasted_iota(I32, (1, qblk), 1)
    n_adm = jnp.minimum(((past + qb * qblk + qcol) // CHUNK + 1) * CHUNK, nkeys)
    n_adm_max = jnp.minimum(((past + (qb + 1) * qblk - 1) // CHUNK + 1) * CHUNK, nkeys)
    nkc = (n_adm_max + kc - 1) // kc
    qit = qi_ref[...].T
    wt = (w_ref[...] * (IDX_HEADS ** -0.5)).T
    qip = [jnp.concatenate([qit[h * IDX_DIM:(h + 1) * IDX_DIM] for h in (2 * j, 2 * j + 1)], axis=1).astype(BF16)
           for j in range(IDX_HEADS // 2)]
    wh = [wt[IDX_DIM + h:IDX_DIM + h + 1] for h in range(IDX_HEADS)]
    kpos = lax.broadcasted_iota(I32, (kc, qblk), 0)

    def score_chunk(c, carry):
        mn, mx = carry
        base = pl.multiple_of(c * kc, kc)
        kic = kib_s[pl.ds(base, kc), :]
        sc = jnp.zeros((kc, qblk), F32)
        for j in range(IDX_HEADS // 2):
            rel = jnp.dot(kic, qip[j], preferred_element_type=F32)
            sc = (sc + wh[2 * j] * jnp.maximum(rel[:, :qblk], 0.0)
                  + wh[2 * j + 1] * jnp.maximum(rel[:, qblk:], 0.0))
        adm = (base + kpos) < n_adm
        sc_s[c] = jnp.where(adm, sc, NEG_INF)
        mn = jnp.minimum(mn, jnp.min(jnp.where(adm, sc, jnp.inf), axis=0, keepdims=True))
        mx = jnp.maximum(mx, jnp.max(jnp.where(adm, sc, -jnp.inf), axis=0, keepdims=True))
        return mn, mx

    mn, mx = lax.fori_loop(0, nkc, score_chunk,
                           (jnp.full((1, qblk), jnp.inf, F32), jnp.full((1, qblk), -jnp.inf, F32)))

    def count(pred):
        def body(c, acc):
            hit = jnp.where(pred(c, sc_s[c]), 1.0, 0.0)
            for i in range(kc // acc_rows):
                acc = acc + hit[i * acc_rows:(i + 1) * acc_rows]
            return acc
        acc = lax.fori_loop(0, nkc, body, jnp.zeros((acc_rows, qblk), F32))
        return jnp.sum(acc, axis=0, keepdims=True)

    cl0 = jnp.where(n_adm <= topk, kf, n_adm.astype(F32))
    hi0 = mx + (jnp.abs(mx) + jnp.abs(mn) + 1e-30) * 1e-6

    def mid_of(lo, hi):
        return lo + (hi - lo) * 0.5

    def bis_cond(cy):
        pending, it = cy[4], cy[5]
        return jnp.logical_and(it < 600, jnp.max(pending) > 0.0)

    def bis_pass(lo, hi, cl, ch):
        mid = mid_of(lo, hi)
        cnt = count(lambda c, s: s >= mid)
        inside = (mid > lo) & (mid < hi)
        up = inside & (cnt >= kf)
        dn = inside & (cnt < kf)
        return jnp.where(up, mid, lo), jnp.where(dn, mid, hi), jnp.where(up, cnt, cl), jnp.where(dn, cnt, ch)

    def bis_body(cy):
        lo, hi, cl, ch, _, it = cy
        for _ in range(BIS_PASSES_PER_CHECK):
            lo, hi, cl, ch = bis_pass(lo, hi, cl, ch)
        above = count(lambda c, s: s > lo)
        mid = mid_of(lo, hi)
        done = (cl == kf) | (above == ch) | jnp.logical_not((mid > lo) & (mid < hi))
        return lo, hi, cl, ch, jnp.where(done, 0.0, 1.0), it + 1

    lo, hi, cl, ch, _, _ = lax.while_loop(
        bis_cond, bis_body,
        (mn, hi0, cl0, jnp.zeros((1, qblk), F32), jnp.where(cl0 == kf, 0.0, 1.0), jnp.int32(0)))

    need = kf - ch

    def tie_limit():
        def bit(i, j):
            jt = j + lax.shift_left(jnp.int32(1), jnp.asarray(jbits - 1 - i, I32))
            cnt = count(lambda c, s: (s >= lo) & (s < hi) & ((c * kc + kpos) < jt))
            return jnp.where(cnt <= need, jt, j)
        return lax.fori_loop(0, jbits, bit, jnp.zeros((1, qblk), I32))

    has_tie = jnp.max(jnp.where(cl > kf, 1.0, 0.0)) > 0.0
    jl = lax.cond(has_tie, tie_limit, lambda: jnp.full((1, qblk), 2 ** 30, I32))

    def mask_chunk(c, carry):
        s = sc_s[c]
        sel = ((s >= hi) | ((s >= lo) & ((c * kc + kpos) < jl))) & (s > 0.5 * NEG_INF)
        sc_s[c] = jnp.where(sel, 0.0, NEG_INF)
        return carry

    lax.fori_loop(0, nkc, mask_chunk, 0)

    qh_s[...] = (q_ref[...] * (DSA_HEAD_DIM ** -0.5)).T.astype(BF16)

    def logit_chunk(c, ms):
        base = pl.multiple_of(c * kc, kc)
        bias = sc_s[c]
        kch = kb_s[pl.ds(base, kc), :]
        out = []
        for h in range(DSA_HEADS):
            n = h // DSA_GROUP
            lg = jnp.dot(kch[:, n * DSA_HEAD_DIM:(n + 1) * DSA_HEAD_DIM],
                         qh_s[h * DSA_HEAD_DIM:(h + 1) * DSA_HEAD_DIM, :], preferred_element_type=F32) + bias
            lg_s[c, h] = lg
            out.append(jnp.maximum(ms[h], jnp.max(lg, axis=0, keepdims=True)))
        return tuple(out)

    ms = lax.fori_loop(0, nkc, logit_chunk, tuple(jnp.full((1, qblk), NEG_INF, F32) for _ in range(DSA_HEADS)))

    def value_chunk(c, carry):
        ls, accs = carry
        ls_new, accs_new = [], []
        for h in range(DSA_HEADS):
            n = h // DSA_GROUP
            p = jnp.exp(lg_s[c, h] - ms[h])
            ls_new.append(ls[h] + jnp.sum(p, axis=0, keepdims=True))
            accs_new.append(accs[h] + jnp.dot(vt_s[c, n * DSA_HEAD_DIM:(n + 1) * DSA_HEAD_DIM, :], p.astype(BF16),
                                              preferred_element_type=F32))
        return tuple(ls_new), tuple(accs_new)

    init = (tuple(jnp.zeros((1, qblk), F32) for _ in range(DSA_HEADS)),
            tuple(jnp.zeros((DSA_HEAD_DIM, qblk), F32) for _ in range(DSA_HEADS)))
    ls, accs = lax.fori_loop(0, nkc, value_chunk, init)
    out = jnp.concatenate([accs[h] / ls[h] for h in range(DSA_HEADS)], axis=0)
    o_ref[...] = out.T


def _dsa(q, qi, kiw, new_keys, cache_keys, nb, t, topk):
    past = cache_keys[0].shape[1] if cache_keys else 0
    kc = DSA_KEY_CHUNK
    lp = ((past + t + kc - 1) // kc) * kc
    qblk = min(t, LANE)
    assert past % kc == 0 and kc >= topk and t % qblk == 0 and qblk % CHUNK == 0
    nq = t // qblk
    jbits = int(math.ceil(math.log2(lp))) + 1

    def qspec(width):
        return pl.BlockSpec((qblk, width), lambda b, i: (b * nq + i, 0))

    def kspec(a):
        return pl.BlockSpec((None,) + a.shape[1:], lambda b, i: (b, 0, 0))

    keys = (tuple(cache_keys) if cache_keys else ()) + tuple(new_keys)
    return pl.pallas_call(
        functools.partial(_dsa_kernel, past=past, nkeys=past + t, topk=topk, kc=kc, qblk=qblk, jbits=jbits),
        grid=(nb, nq),
        in_specs=[qspec(DSA_WIDTH), qspec(IDX_HEADS * IDX_DIM), qspec(LANE)] + [kspec(a) for a in keys],
        out_specs=qspec(DSA_WIDTH),
        out_shape=jax.ShapeDtypeStruct((nb * t, DSA_WIDTH), F32),
        scratch_shapes=[pltpu.VMEM((lp, LANE), BF16), pltpu.VMEM((lp // kc, LANE, kc), BF16),
                        pltpu.VMEM((lp, IDX_DIM), BF16), pltpu.VMEM((lp // kc, kc, qblk), F32),
                        pltpu.VMEM((DSA_WIDTH, qblk), BF16), pltpu.VMEM((lp // kc, DSA_HEADS, kc, qblk), F32)],
        compiler_params=_cparams(("parallel", "arbitrary")),
        name="dsa",
    )(q, qi, kiw, *keys)


def _mem_kernel(q_ref, mk_ref, mv_ref, o_ref):
    scale = MEM_HEAD_DIM ** -0.5
    for h in range(MEM_HEADS):
        sl = slice(h * MEM_HEAD_DIM, (h + 1) * MEM_HEAD_DIM)
        qh = q_ref[:, sl].astype(BF16)
        kh = mk_ref[:, sl].astype(BF16)
        vh = mv_ref[:, sl].astype(BF16)
        lg = lax.dot_general(qh, kh, NT_DIMS, preferred_element_type=F32) * scale
        p = jnp.exp(lg - jnp.max(lg, axis=-1, keepdims=True))
        o = jnp.dot(p.astype(BF16), vh, preferred_element_type=F32)
        o_ref[:, sl] = o / jnp.sum(p, axis=-1, keepdims=True)


def _mem_attn(h, mk, mv, kcol, vcol, row0, nb, t):
    tq = _tile(t, 256)
    nt = t // tq
    rb0 = row0 // tq
    m = mk.shape[1]
    return pl.pallas_call(
        _mem_kernel,
        grid=(nb, nt),
        in_specs=[pl.BlockSpec((tq, MEM_WIDTH), lambda b, i: (rb0 + b * nt + i, C_MQ // MEM_WIDTH)),
                  pl.BlockSpec((None, m, MEM_WIDTH), lambda b, i: (b, 0, kcol)),
                  pl.BlockSpec((None, m, MEM_WIDTH), lambda b, i: (b, 0, vcol))],
        out_specs=pl.BlockSpec((tq, MEM_WIDTH), lambda b, i: (b * nt + i, 0)),
        out_shape=jax.ShapeDtypeStruct((nb * t, MEM_WIDTH), F32),
        compiler_params=_cparams(("parallel", "parallel")),
        name="mem_attn",
    )(h, mk, mv)


def _merge_kernel(*refs, alpha, first_tiles):
    br_a, br_b = refs[0:N_BRANCH], refs[N_BRANCH:2 * N_BRANCH]
    g_refs = refs[2 * N_BRANCH:3 * N_BRANCH]
    x_ref, wb_ref, wo_ref, lg_ref, lb_ref, o_ref = refs[3 * N_BRANCH:]
    in_first = pl.program_id(0) < first_tiles
    merged = None
    for n in range(N_BRANCH):
        br = jnp.where(in_first, br_a[n][...], br_b[n][...])
        proj = jnp.dot(br.astype(BF16), wb_ref[n], preferred_element_type=F32)
        term = _sigmoid(g_refs[n][...]) * proj
        merged = term if merged is None else merged + term
    m = jnp.dot(merged.astype(BF16), wo_ref[...], preferred_element_type=F32)
    o_ref[...] = _layer_norm(alpha * x_ref[...] + m, lg_ref[...], lb_ref[...])


def _merge(branches_a, branches_b, h, x, wb, wo, lng, lnb, alpha):
    n = x.shape[0]
    rows_a, rows_b = branches_a[0].shape[0], branches_b[0].shape[0]
    tm = _tile(math.gcd(rows_a, rows_b), 256)
    ta, tb = rows_a // tm, rows_b // tm
    assert rows_a + rows_b == n
    aspec = pl.BlockSpec((tm, BRANCH_WIDTH), lambda i: (jnp.minimum(i, ta - 1), 0))
    bspec = pl.BlockSpec((tm, BRANCH_WIDTH), lambda i: (jnp.maximum(i - ta, 0), 0))

    def gspec(j):
        return pl.BlockSpec((tm, D_MODEL), lambda i: (i, C_GATE // D_MODEL + j))

    xspec = pl.BlockSpec((tm, D_MODEL), lambda i: (i, 0))
    vspec = pl.BlockSpec((1, D_MODEL), lambda i: (0, 0))
    return pl.pallas_call(
        functools.partial(_merge_kernel, alpha=alpha, first_tiles=ta),
        grid=(ta + tb,),
        in_specs=[aspec] * N_BRANCH + [bspec] * N_BRANCH + [gspec(0), gspec(1), gspec(2), xspec,
                  pl.BlockSpec((N_BRANCH, BRANCH_WIDTH, D_MODEL), lambda i: (0, 0, 0)),
                  pl.BlockSpec((D_MODEL, D_MODEL), lambda i: (0, 0)), vspec, vspec],
        out_specs=xspec,
        out_shape=jax.ShapeDtypeStruct((n, D_MODEL), F32),
        compiler_params=_cparams(("arbitrary",)),
        name="merge",
    )(*branches_a, *branches_b, h, h, h, x, wb, wo, lng, lnb)


def _swiglu_partial(xb, wg_ref, wu_ref, wd_ref):
    g = jnp.dot(xb, wg_ref[...], preferred_element_type=F32)
    u = jnp.dot(xb, wu_ref[...], preferred_element_type=F32)
    a = (g * _sigmoid(g)) * u
    return jnp.dot(a.astype(BF16), wd_ref[...], preferred_element_type=F32)


def _ffn_kernel(x_ref, wg_ref, wu_ref, wd_ref, lg_ref, lb_ref, o_ref, xb_s, acc_s, *, alpha):
    k = pl.program_id(1)

    @pl.when(k == 0)
    def _():
        xb_s[...] = x_ref[...].astype(BF16)
        acc_s[...] = jnp.zeros_like(acc_s)

    acc_s[...] += _swiglu_partial(xb_s[...], wg_ref, wu_ref, wd_ref)

    @pl.when(k == pl.num_programs(1) - 1)
    def _():
        o_ref[...] = _layer_norm(alpha * x_ref[...] + acc_s[...], lg_ref[...], lb_ref[...])


def _ffn_dense(x, w_gu, w_down, lng, lnb, alpha):
    n = x.shape[0]
    f = w_down.shape[0]
    tm = _tile(n, 512)
    fc = _tile(f, 1536, LANE)
    nk = f // fc
    xspec = pl.BlockSpec((tm, D_MODEL), lambda i, k: (i, 0))
    vspec = pl.BlockSpec((1, D_MODEL), lambda i, k: (0, 0))
    return pl.pallas_call(
        functools.partial(_ffn_kernel, alpha=alpha),
        grid=(n // tm, nk),
        in_specs=[xspec,
                  pl.BlockSpec((D_MODEL, fc), lambda i, k: (0, k)),
                  pl.BlockSpec((D_MODEL, fc), lambda i, k: (0, nk + k)),
                  pl.BlockSpec((fc, D_MODEL), lambda i, k: (k, 0)), vspec, vspec],
        out_specs=xspec,
        out_shape=jax.ShapeDtypeStruct((n, D_MODEL), F32),
        scratch_shapes=[pltpu.VMEM((tm, D_MODEL), BF16), pltpu.VMEM((tm, D_MODEL), F32)],
        compiler_params=_cparams(("parallel", "arbitrary")),
        name="ffn_dense",
    )(x, w_gu, w_gu, w_down, lng, lnb)


def _router_kernel(x_ref, w_ref, e_ref, g_ref):
    x = x_ref[...]
    w = w_ref[...]
    xh = x.astype(BF16)
    xl = (x - xh.astype(F32)).astype(BF16)
    wh = w.astype(BF16)
    wl = (w - wh.astype(F32)).astype(BF16)
    lg = (jnp.dot(xh, wh, preferred_element_type=F32) + jnp.dot(xh, wl, preferred_element_type=F32)
          + jnp.dot(xl, wh, preferred_element_type=F32))
    lane = lax.broadcasted_iota(I32, lg.shape, 1).astype(F32)
    lg = jnp.where(lane < N_EXPERTS, lg, -jnp.inf)
    m1 = jnp.max(lg, axis=-1, keepdims=True)
    i1 = jnp.min(jnp.where(lg == m1, lane, float(LANE)), axis=-1, keepdims=True)
    lg2 = jnp.where(lane == i1, -jnp.inf, lg)
    m2 = jnp.max(lg2, axis=-1, keepdims=True)
    i2 = jnp.min(jnp.where(lg2 == m2, lane, float(LANE)), axis=-1, keepdims=True)
    e = jnp.exp(m2 - m1)
    g1 = 1.0 / (1.0 + e)
    g2 = e / (1.0 + e)
    e_ref[...] = jnp.where(lane == 0.0, i1, jnp.where(lane == 1.0, i2, 0.0)).astype(I32)
    g_ref[...] = jnp.where(lane == 0.0, g1, jnp.where(lane == 1.0, g2, 0.0))


def _router(x, w_router):
    n = x.shape[0]
    tm = _tile(n, 512)
    wp = jnp.pad(w_router, ((0, 0), (0, LANE - N_EXPERTS)))
    ospec = pl.BlockSpec((tm, LANE), lambda i: (i, 0))
    return pl.pallas_call(
        _router_kernel,
        grid=(n // tm,),
        in_specs=[pl.BlockSpec((tm, D_MODEL), lambda i: (i, 0)), pl.BlockSpec((D_MODEL, LANE), lambda i: (0, 0))],
        out_specs=[ospec, ospec],
        out_shape=[jax.ShapeDtypeStruct((n, LANE), I32), jax.ShapeDtypeStruct((n, LANE), F32)],
        compiler_params=_cparams(("parallel",)),
        name="router",
    )(x, wp)


def _row_copy(src, dst, s, d, sem):
    return pltpu.make_async_copy(src.at[pl.ds(s, 1)], dst.at[pl.ds(d, 1)], sem)


def _dispatch_kernel(pos_ref, x_ref, xs_in, xs_hbm, sem, *, tb):
    del xs_in
    i = pl.program_id(0)

    def start(r, c):
        t = i * tb + r
        for j in range(TOP_K):
            _row_copy(x_ref, xs_hbm, r, pos_ref[TOP_K * t + j], sem).start()
        return c

    lax.fori_loop(0, tb, start, 0)
    for j in range(TOP_K):
        pltpu.make_async_copy(x_ref, xs_hbm.at[pl.ds(0, tb)], sem).wait()


def _dispatch(pos, x, n_rows):
    n = x.shape[0]
    tb = _tile(n, 256)
    xs0 = jnp.zeros((n_rows, D_MODEL), F32)
    return pl.pallas_call(
        functools.partial(_dispatch_kernel, tb=tb),
        grid_spec=pltpu.PrefetchScalarGridSpec(
            num_scalar_prefetch=1, grid=(n // tb,),
            in_specs=[pl.BlockSpec((tb, D_MODEL), lambda i, p: (i, 0)), pl.BlockSpec(memory_space=pl.ANY)],
            out_specs=pl.BlockSpec(memory_space=pl.ANY),
            scratch_shapes=[pltpu.SemaphoreType.DMA(())]),
        out_shape=jax.ShapeDtypeStruct((n_rows, D_MODEL), F32),
        input_output_aliases={2: 0},
        compiler_params=pltpu.CompilerParams(dimension_semantics=("arbitrary",), has_side_effects=True),
        name="moe_dispatch",
    )(pos, x, xs0)


def _expert_kernel(te_ref, nu_ref, x_ref, wg_ref, wu_ref, wd_ref, o_ref, xb_s, acc_s):
    del te_ref
    i = pl.program_id(0)
    k = pl.program_id(1)
    used = i < nu_ref[0]

    @pl.when(jnp.logical_and(used, k == 0))
    def _():
        xb_s[...] = x_ref[...].astype(BF16)
        acc_s[...] = jnp.zeros_like(acc_s)

    @pl.when(used)
    def _():
        acc_s[...] += _swiglu_partial(xb_s[...], wg_ref, wu_ref, wd_ref)

    @pl.when(k == pl.num_programs(1) - 1)
    def _():
        o_ref[...] = jnp.where(used, acc_s[...], 0.0)


def _experts(tile_expert, n_used, xs, w_gu, w_down, tm):
    n_rows = xs.shape[0]
    f = w_down.shape[1]
    fc = _tile(f, 1024, LANE)
    nk = f // fc
    xspec = pl.BlockSpec((tm, D_MODEL), lambda i, k, te, nu: (i, 0))
    return pl.pallas_call(
        _expert_kernel,
        grid_spec=pltpu.PrefetchScalarGridSpec(
            num_scalar_prefetch=2, grid=(n_rows // tm, nk),
            in_specs=[xspec,
                      pl.BlockSpec((None, D_MODEL, fc), lambda i, k, te, nu: (te[i], 0, k)),
                      pl.BlockSpec((None, D_MODEL, fc), lambda i, k, te, nu: (te[i], 0, nk + k)),
                      pl.BlockSpec((None, fc, D_MODEL), lambda i, k, te, nu: (te[i], k, 0))],
            out_specs=xspec,
            scratch_shapes=[pltpu.VMEM((tm, D_MODEL), BF16), pltpu.VMEM((tm, D_MODEL), F32)]),
        out_shape=jax.ShapeDtypeStruct((n_rows, D_MODEL), F32),
        compiler_params=_cparams(("arbitrary", "arbitrary")),
        name="moe_experts",
    )(tile_expert, n_used, xs, w_gu, w_gu, w_down)


def _combine_kernel(pos_ref, ys_hbm, g_ref, x_ref, lg_ref, lb_ref, o_ref, buf, sem, *, tb, alpha):
    i = pl.program_id(0)

    def start(r, c):
        t = i * tb + r
        for j in range(TOP_K):
            _row_copy(ys_hbm, buf.at[j], pos_ref[TOP_K * t + j], r, sem.at[j]).start()
        return c

    lax.fori_loop(0, tb, start, 0)
    for j in range(TOP_K):
        pltpu.make_async_copy(ys_hbm.at[pl.ds(0, tb)], buf.at[j], sem.at[j]).wait()
    g = g_ref[...]
    f = g[:, 0:1] * buf[0] + g[:, 1:2] * buf[1]
    o_ref[...] = _layer_norm(alpha * x_ref[...] + f, lg_ref[...], lb_ref[...])


def _combine(pos, ys, gates, x, lng, lnb, alpha):
    n = x.shape[0]
    tb = _tile(n, 256)
    xspec = pl.BlockSpec((tb, D_MODEL), lambda i, p: (i, 0))
    vspec = pl.BlockSpec((1, D_MODEL), lambda i, p: (0, 0))
    return pl.pallas_call(
        functools.partial(_combine_kernel, tb=tb, alpha=alpha),
        grid_spec=pltpu.PrefetchScalarGridSpec(
            num_scalar_prefetch=1, grid=(n // tb,),
            in_specs=[pl.BlockSpec(memory_space=pl.ANY), pl.BlockSpec((tb, LANE), lambda i, p: (i, 0)),
                      xspec, vspec, vspec],
            out_specs=xspec,
            scratch_shapes=[pltpu.VMEM((TOP_K, tb, D_MODEL), F32), pltpu.SemaphoreType.DMA((TOP_K,))]),
        out_shape=jax.ShapeDtypeStruct((n, D_MODEL), F32),
        compiler_params=_cparams(("arbitrary",)),
        name="moe_combine",
    )(pos, ys, gates, x, lng, lnb)


def _moe(x, w_router, w_gu, w_down, lng, lnb, alpha):
    n = x.shape[0]
    tm = _tile(n, 512)
    eidx, gates = _router(x, w_router)
    e2 = eidx[:, :TOP_K].reshape(-1)
    onehot = (e2[:, None] == jnp.arange(N_EXPERTS, dtype=I32)[None, :]).astype(I32)
    csum = jnp.cumsum(onehot, axis=0)
    rank = jnp.take_along_axis(csum, e2[:, None], axis=1)[:, 0] - 1
    counts = csum[-1]
    padded = ((counts + tm - 1) // tm) * tm
    ends = jnp.cumsum(padded)
    pos = ((ends - padded)[e2] + rank).astype(I32)
    n_rows = TOP_K * n + N_EXPERTS * tm
    n_tiles = n_rows // tm
    tile_start = jnp.arange(n_tiles, dtype=I32) * tm
    tile_expert = jnp.minimum(jnp.sum((tile_start[:, None] >= ends[None, :]).astype(I32), axis=1),
                              N_EXPERTS - 1).astype(I32)
    n_used = (ends[-1:] // tm).astype(I32)
    xs = _dispatch(pos, x, n_rows)
    ys = _experts(tile_expert, n_used, xs, w_gu, w_down, tm)
    return _combine(pos, ys, gates, x, lng, lnb, alpha)


def _pack_w_in(w):
    d = w.shape[0]
    parts, start = [], 0
    for n in IN_SIZES:
        parts.append(w[:, start:start + n])
        start += n
    hq, hf, hi, hg, dq, dk, dv, iq, ik, iw, mq, gates = parts
    ikw = jnp.concatenate([ik, iw, jnp.zeros((d, LANE - IDX_DIM - IDX_HEADS), w.dtype)], axis=1)
    tail = jnp.zeros((d, H_COLS - (C_IK + LANE)), w.dtype)
    return jnp.concatenate([hq, hf, hi, hg, gates, dq, mq, iq, dk, dv, ikw, tail], axis=1).astype(BF16)


def kernel(x_prompt, x_sample, cache_dsa_k, cache_dsa_v, cache_idx_k, state_hgrn, cache_mem_k, cache_mem_v,
           mem_prompt, w_in, hgrn_lb_logits, hgrn_norm_g, w_branch, w_out, w_mem_kv, ln_g, ln_b, ffn_w_gate_up,
           ffn_w_down, moe_router, moe_w_gate_up, moe_w_down):
    depth = w_in.shape[0]
    alpha = (2 * depth) ** 0.25
    bp, tp, d = x_prompt.shape
    bs, ts, _ = x_sample.shape
    past = cache_dsa_k.shape[2]
    m = mem_prompt.shape[1]
    np_rows, ns_rows = bp * tp, bs * ts

    sm = jax.nn.softmax(hgrn_lb_logits.astype(F32), axis=0)
    lbs = jnp.cumsum(sm, axis=0) - sm[0:1]

    x = jnp.concatenate([x_prompt.reshape(np_rows, d), x_sample.reshape(ns_rows, d)], axis=0)
    n = x.shape[0]
    tables_p = _rope_tables(jnp.arange(tp, dtype=I32))
    tables_s = _rope_tables(past + jnp.arange(ts, dtype=I32))
    mem_flat = mem_prompt.reshape(bp * m, d)
    zeros_state = jnp.zeros((bp, HG_HEADS, HG_DK, HG_DV), F32)
    outs = {k: [] for k in ("pk", "pv", "pki", "ph", "pmk", "pmv", "sk", "sv", "ski", "sh")}
    for l in range(depth):
        h = _matmul(x, _pack_w_in(w_in[l]), _tile(n, 512), IN_TN, "in_proj")
        mem_kv = _matmul(mem_flat, w_mem_kv[l].astype(BF16), _tile(bp * m, 512), 512, "mem_kv").reshape(bp, m, 2 * MEM_WIDTH)
        lb = lbs[l].reshape(1, HG_WIDTH)
        ng = hgrn_norm_g[l].reshape(1, HG_WIDTH).astype(F32)

        qp, kp, qip, kip = _rope(h, tables_p, 0, bp, tp)
        vp = h[:np_rows, C_DV:C_DV + LANE]
        kip32 = kip[:, :IDX_DIM]
        ohg_p, sh_p = _hgrn(h, lb, ng, zeros_state, 0, bp, tp)
        odsa_p = _dsa(qp, qip, kip,
                      (kp.reshape(bp, tp, LANE), vp.reshape(bp, tp, LANE), kip32.reshape(bp, tp, IDX_DIM)), None,
                      bp, tp, min(TOPK_MAX, tp // 4))
        omem_p = _mem_attn(h, mem_kv, mem_kv, 0, 1, 0, bp, tp)

        qs, ks, qis, kis = _rope(h, tables_s, np_rows, bs, ts)
        vs = h[np_rows:, C_DV:C_DV + LANE]
        kis32 = kis[:, :IDX_DIM]
        ohg_s, sh_s = _hgrn(h, lb, ng, state_hgrn[l], np_rows, bs, ts)
        odsa_s = _dsa(qs, qis, kis,
                      (ks.reshape(bs, ts, LANE), vs.reshape(bs, ts, LANE), kis32.reshape(bs, ts, IDX_DIM)),
                      (cache_dsa_k[l].reshape(bs, past, LANE), cache_dsa_v[l].reshape(bs, past, LANE), cache_idx_k[l]),
                      bs, ts, min(TOPK_MAX, (past + ts) // 4))
        omem_s = _mem_attn(h, cache_mem_k[l].reshape(bs, m, MEM_WIDTH), cache_mem_v[l].reshape(bs, m, MEM_WIDTH),
                           0, 0, np_rows, bs, ts)

        x = _merge((ohg_p, odsa_p, omem_p), (ohg_s, odsa_s, omem_s), h, x, w_branch[l].astype(BF16),
                   w_out[l].astype(BF16), ln_g[l, 0].reshape(1, d), ln_b[l, 0].reshape(1, d), alpha)
        if l % 2 == 0:
            x = _ffn_dense(x, ffn_w_gate_up[l // 2].astype(BF16), ffn_w_down[l // 2].astype(BF16),
                           ln_g[l, 1].reshape(1, d), ln_b[l, 1].reshape(1, d), alpha)
        else:
            x = _moe(x, moe_router[l // 2], moe_w_gate_up[l // 2].astype(BF16), moe_w_down[l // 2].astype(BF16),
                     ln_g[l, 1].reshape(1, d), ln_b[l, 1].reshape(1, d), alpha)

        outs["pk"].append(kp.reshape(bp, tp, DSA_KV_HEADS, DSA_HEAD_DIM))
        outs["pv"].append(vp.reshape(bp, tp, DSA_KV_HEADS, DSA_HEAD_DIM))
        outs["pki"].append(kip32.reshape(bp, tp, IDX_DIM))
        outs["ph"].append(sh_p)
        outs["pmk"].append(mem_kv[..., :MEM_WIDTH].reshape(bp, m, MEM_HEADS, MEM_HEAD_DIM))
        outs["pmv"].append(mem_kv[..., MEM_WIDTH:].reshape(bp, m, MEM_HEADS, MEM_HEAD_DIM))
        outs["sk"].append(ks.reshape(bs, ts, DSA_KV_HEADS, DSA_HEAD_DIM))
        outs["sv"].append(vs.reshape(bs, ts, DSA_KV_HEADS, DSA_HEAD_DIM))
        outs["ski"].append(kis32.reshape(bs, ts, IDX_DIM))
        outs["sh"].append(sh_s)

    st = {k: jnp.stack(v) for k, v in outs.items()}
    y_prompt = x[:np_rows].reshape(bp, tp, d)
    y_sample = x[np_rows:].reshape(bs, ts, d)
    return (y_prompt, y_sample, st["pk"], st["pv"], st["pki"], st["ph"], st["pmk"], st["pmv"],
            st["sk"], st["sv"], st["ski"], st["sh"])
```

```python
import functools
import math

import jax
import jax.numpy as jnp
from jax import lax
from jax.experimental import pallas as pl
from jax.experimental.pallas import tpu as pltpu

F32, BF16, I32 = jnp.float32, jnp.bfloat16, jnp.int32

D_MODEL = 1024
CHUNK = 64
HG_HEADS, HG_DK, HG_DV, HG_BLOCK = 4, 128, 128, 32
HG_WIDTH = HG_HEADS * HG_DV
LB_FLOOR = 1e-30
DSA_HEADS, DSA_KV_HEADS, DSA_HEAD_DIM = 8, 2, 64
DSA_GROUP = DSA_HEADS // DSA_KV_HEADS
DSA_WIDTH = DSA_HEADS * DSA_HEAD_DIM
IDX_HEADS, IDX_DIM = 8, 32
TOPK_MAX = 256
Q_BLOCK = 64
MEM_HEADS, MEM_HEAD_DIM = 4, 128
MEM_WIDTH = MEM_HEADS * MEM_HEAD_DIM
N_BRANCH, BRANCH_WIDTH = 3, 512
N_EXPERTS, TOP_K = 8, 2
ROPE_THETA = 10000.0
LN_EPS = 1e-5
RMS_EPS = 1e-6
NEG_INF = -1e30
IN_SIZES = (HG_HEADS * HG_DK, HG_HEADS * HG_DK, HG_HEADS * HG_DV, HG_WIDTH,
            DSA_WIDTH, DSA_KV_HEADS * DSA_HEAD_DIM, DSA_KV_HEADS * DSA_HEAD_DIM,
            IDX_HEADS * IDX_DIM, IDX_DIM, IDX_HEADS, MEM_WIDTH, N_BRANCH * D_MODEL)

LANE = 128
SUBLANE = 8
VMEM_LIMIT = 48 * 1024 * 1024

C_HQ, C_HF, C_HI, C_HG = 0, 512, 1024, 1536
C_GATE = 2048
C_DQ = 5120
C_MQ = 5632
C_IQ = 6144
C_DK = 6400
C_DV = 6528
C_IK = 6656
H_COLS = 6912
IN_TN = 768

DSA_KEY_CHUNK = 256
BIS_PASSES_PER_CHECK = 4

NT_DIMS = (((1,), (1,)), ((), ()))


def _cparams(sem):
    return pltpu.CompilerParams(dimension_semantics=sem, vmem_limit_bytes=VMEM_LIMIT)


def _tile(n, pref, mult=SUBLANE):
    t = min(n, pref)
    while n % t or t % mult:
        t -= 1
    return t


def _sigmoid(x):
    return 1.0 / (1.0 + jnp.exp(-x))


def _layer_norm(v, g, b):
    mu = jnp.mean(v, axis=-1, keepdims=True)
    d = v - mu
    var = jnp.mean(d * d, axis=-1, keepdims=True)
    return d * lax.rsqrt(var + LN_EPS) * g + b


def _mm_kernel(x_ref, w_ref, o_ref):
    o_ref[...] = jnp.dot(x_ref[...].astype(BF16), w_ref[...], preferred_element_type=F32)


def _matmul(x, w, tm, tn, name):
    n, k = x.shape
    nc = w.shape[1]
    return pl.pallas_call(
        _mm_kernel,
        grid=(n // tm, nc // tn),
        in_specs=[pl.BlockSpec((tm, k), lambda i, j: (i, 0)),
                  pl.BlockSpec((k, tn), lambda i, j: (0, j))],
        out_specs=pl.BlockSpec((tm, tn), lambda i, j: (i, j)),
        out_shape=jax.ShapeDtypeStruct((n, nc), F32),
        compiler_params=_cparams(("parallel", "arbitrary")),
        name=name,
    )(x, w)


def _rot_half(x, half):
    lane = lax.broadcasted_iota(I32, x.shape, 1)
    lo = (lane % (2 * half)) < half
    return jnp.where(lo, pltpu.roll(x, LANE - half, 1), pltpu.roll(x, half, 1))


def _rope_kernel(dq_ref, dk_ref, iq_ref, ik_ref, c64_ref, s64_ref, c32_ref, s32_ref, cik_ref, sik_ref,
                 q_ref, k_ref, qi_ref, ki_ref):
    c64, s64 = c64_ref[...], s64_ref[...]
    c32, s32 = c32_ref[...], s32_ref[...]
    for j in range(DSA_WIDTH // LANE):
        x = dq_ref[:, j * LANE:(j + 1) * LANE]
        q_ref[:, j * LANE:(j + 1) * LANE] = x * c64 + _rot_half(x, DSA_HEAD_DIM // 2) * s64
    x = dk_ref[...]
    k_ref[...] = x * c64 + _rot_half(x, DSA_HEAD_DIM // 2) * s64
    for j in range(IDX_HEADS * IDX_DIM // LANE):
        x = iq_ref[:, j * LANE:(j + 1) * LANE]
        qi_ref[:, j * LANE:(j + 1) * LANE] = x * c32 + _rot_half(x, IDX_DIM // 2) * s32
    x = ik_ref[...]
    ki_ref[...] = x * cik_ref[...] + _rot_half(x, IDX_DIM // 2) * sik_ref[...]


def _rope_tables(pos):
    lane = jnp.arange(LANE)
    p = pos.astype(F32)[:, None]
    h64 = DSA_HEAD_DIM // 2
    inv64 = ROPE_THETA ** (-jnp.arange(h64, dtype=F32) / h64)
    a64 = p * inv64[lane % h64][None, :]
    c64 = jnp.cos(a64)
    s64 = jnp.sin(a64) * jnp.where((lane % DSA_HEAD_DIM) < h64, -1.0, 1.0)[None, :]
    h32 = IDX_DIM // 2
    inv32 = ROPE_THETA ** (-jnp.arange(h32, dtype=F32) / h32)
    a32 = p * inv32[lane % h32][None, :]
    c32 = jnp.cos(a32)
    s32 = jnp.sin(a32) * jnp.where((lane % IDX_DIM) < h32, -1.0, 1.0)[None, :]
    live = (lane < IDX_DIM)[None, :]
    cik = jnp.where(live, c32, 1.0)
    sik = jnp.where(live, s32, 0.0)
    return tuple(t.astype(F32) for t in (c64, s64, c32, s32, cik, sik))


def _rope(h, tables, row0, nb, t):
    tr = _tile(t, 256)
    nt = t // tr
    rb0 = row0 // tr
    rows = nb * t

    def hspec(width, col):
        return pl.BlockSpec((tr, width), lambda i: (rb0 + i, col // width))

    tspec = pl.BlockSpec((tr, LANE), lambda i: (i % nt, 0))

    def ospec(width):
        return pl.BlockSpec((tr, width), lambda i: (i, 0))

    return pl.pallas_call(
        _rope_kernel,
        grid=(rows // tr,),
        in_specs=[hspec(DSA_WIDTH, C_DQ), hspec(LANE, C_DK), hspec(IDX_HEADS * IDX_DIM, C_IQ), hspec(LANE, C_IK)]
        + [tspec] * 6,
        out_specs=[ospec(DSA_WIDTH), ospec(LANE), ospec(IDX_HEADS * IDX_DIM), ospec(LANE)],
        out_shape=[jax.ShapeDtypeStruct((rows, DSA_WIDTH), F32), jax.ShapeDtypeStruct((rows, LANE), F32),
                   jax.ShapeDtypeStruct((rows, IDX_HEADS * IDX_DIM), F32), jax.ShapeDtypeStruct((rows, LANE), F32)],
        compiler_params=_cparams(("parallel",)),
        name="rope",
    )(h, h, h, h, *tables)


def _hgrn_kernel(q_ref, f_ref, i_ref, g_ref, lb_ref, ng_ref, s0_ref, o_ref, s_ref,
                 st_s, b_s, k_s, v_s, q_s, o_s, *, tt):
    t = pl.program_id(1)
    c = HG_BLOCK

    @pl.when(t == 0)
    def _():
        for hd in range(HG_HEADS):
            st_s[hd] = s0_ref[hd].T

    r = lax.broadcasted_iota(I32, (tt, tt), 0)
    cc = lax.broadcasted_iota(I32, (tt, tt), 1)
    tri = jnp.where(((r // c) == (cc // c)) & (cc <= r), 1.0, 0.0).astype(BF16)
    for hd in range(HG_HEADS):
        hl = slice(hd * LANE, (hd + 1) * LANE)
        lb = lb_ref[:, hl]
        z = f_ref[:, hl]
        lf = jnp.log(jnp.maximum(lb, LB_FLOOR) + (1.0 - lb) * _sigmoid(z))
        k_s[hd] = (1.0 - lb) * _sigmoid(-z)
        hq = q_ref[:, hl]
        q_s[hd] = hq * _sigmoid(hq)
        v_s[hd] = i_ref[:, hl]
        hi = lf.astype(BF16)
        r1 = lf - hi.astype(F32)
        mid = r1.astype(BF16)
        lo = (r1 - mid.astype(F32)).astype(BF16)
        b_s[hd] = (jnp.dot(tri, hi, preferred_element_type=F32) + jnp.dot(tri, mid, preferred_element_type=F32)
                   + jnp.dot(tri, lo, preferred_element_type=F32))

    row8 = lax.broadcasted_iota(I32, (SUBLANE, LANE), 0)
    nsub = c // SUBLANE

    def block(j, carry):
        base = pl.multiple_of(j * c, c)
        for hd in range(HG_HEADS):
            hl = slice(hd * LANE, (hd + 1) * LANE)
            bj = b_s[hd, pl.ds(base, c), :]
            qj = q_s[hd, pl.ds(base, c), :]
            kj = k_s[hd, pl.ds(base, c), :]
            vj = v_s[hd, pl.ds(base, c), :]
            blast = b_s[hd, pl.ds(base + c - 1, 1), :]
            oc = [jnp.zeros((SUBLANE, LANE), F32) for _ in range(nsub)]
            for s in range(c):
                bs = b_s[hd, pl.ds(base + s, 1), :]
                ks = k_s[hd, pl.ds(base + s, 1), :]
                vs = v_s[hd, pl.ds(base + s, 1), :]
                for u in range(s // SUBLANE, nsub):
                    d = bj[u * SUBLANE:(u + 1) * SUBLANE] - bs
                    if u == s // SUBLANE:
                        d = jnp.where(row8 >= (s % SUBLANE), d, NEG_INF)
                    a = jnp.sum(qj[u * SUBLANE:(u + 1) * SUBLANE] * jnp.exp(d) * ks, axis=-1, keepdims=True)
                    oc[u] = oc[u] + a * vs
            o_intra = jnp.concatenate(oc, axis=0)
            st = st_s[hd]
            qe = qj * jnp.exp(bj)
            o_inter = lax.dot_general(qe.astype(BF16), st.astype(BF16), NT_DIMS, preferred_element_type=F32)
            ke = kj * jnp.exp(blast - bj)
            upd = jnp.dot(vj.T.astype(BF16), ke.astype(BF16), preferred_element_type=F32)
            st_s[hd] = jnp.exp(blast) * st + upd
            o_s[pl.ds(base, c), hl] = o_intra + o_inter
        return carry

    lax.fori_loop(0, tt // c, block, 0)

    for hd in range(HG_HEADS):
        hl = slice(hd * LANE, (hd + 1) * LANE)
        o = o_s[:, hl]
        o = o * lax.rsqrt(jnp.mean(o * o, axis=-1, keepdims=True) + RMS_EPS)
        o_ref[:, hl] = o * ng_ref[:, hl] * _sigmoid(g_ref[:, hl])

    @pl.when(t == pl.num_programs(1) - 1)
    def _():
        for hd in range(HG_HEADS):
            s_ref[hd] = st_s[hd].T


def _hgrn(h, lb, ng, s0, row0, nb, t):
    assert t % HG_BLOCK == 0
    tt = _tile(t, 256, HG_BLOCK)
    nt = t // tt
    rb0 = row0 // tt

    def hspec(col):
        return pl.BlockSpec((tt, HG_WIDTH), lambda b, i: (rb0 + b * nt + i, col // HG_WIDTH))

    vspec = pl.BlockSpec((1, HG_WIDTH), lambda b, i: (0, 0))
    sspec = pl.BlockSpec((None, HG_HEADS, HG_DK, HG_DV), lambda b, i: (b, 0, 0, 0))
    return pl.pallas_call(
        functools.partial(_hgrn_kernel, tt=tt),
        grid=(nb, nt),
        in_specs=[hspec(C_HQ), hspec(C_HF), hspec(C_HI), hspec(C_HG), vspec, vspec, sspec],
        out_specs=[pl.BlockSpec((tt, HG_WIDTH), lambda b, i: (b * nt + i, 0)), sspec],
        out_shape=[jax.ShapeDtypeStruct((nb * t, HG_WIDTH), F32),
                   jax.ShapeDtypeStruct((nb, HG_HEADS, HG_DK, HG_DV), F32)],
        scratch_shapes=[pltpu.VMEM((HG_HEADS, HG_DV, HG_DK), F32)] + [pltpu.VMEM((HG_HEADS, tt, LANE), F32)] * 4
        + [pltpu.VMEM((tt, HG_WIDTH), F32)],
        compiler_params=_cparams(("parallel", "arbitrary")),
        name="hgrn2",
    )(h, h, h, h, lb, ng, s0)


def _dsa_kernel(q_ref, qi_ref, w_ref, *refs, past, nkeys, topk, kc, qblk, jbits):
    n_in = 6 if past else 3
    key_refs, (o_ref, kb_s, vt_s, kib_s, sc_s, qh_s, lg_s) = refs[:n_in], refs[n_in:]
    qb = pl.program_id(1)
    kf = float(topk)
    acc_rows = 4 * SUBLANE
    lp = kb_s.shape[0]
    segs = [(past, nkeys - past) + tuple(key_refs[-3:])]
    if past:
        segs.insert(0, (0, past) + tuple(key_refs[:3]))

    @pl.when(qb == 0)
    def _():
        for s0, n, k_ref, _, ki_ref in segs:
            kb_s[s0:s0 + n, :] = k_ref[...].astype(BF16)
            kib_s[s0:s0 + n, :] = ki_ref[...].astype(BF16)
        if lp > nkeys:
            kb_s[nkeys:, :] = jnp.zeros((lp - nkeys, LANE), BF16)
            kib_s[nkeys:, :] = jnp.zeros((lp - nkeys, IDX_DIM), BF16)
        for c in range(lp // kc):
            pieces = []
            for s0, n, _, v_ref, _ in segs:
                a, b = max(c * kc, s0), min((c + 1) * kc, s0 + n)
                if a < b:
                    pieces.append(v_ref[a - s0:b - s0, :])
            have = sum(p.shape[0] for p in pieces)
            if have < kc:
                pieces.append(jnp.zeros((kc - have, LANE), F32))
            vt_s[c] = (pieces[0] if len(pieces) == 1 else jnp.concatenate(pieces, axis=0)).T.astype(BF16)

    qcol = lax.broadcasted_iota(I32, (1, qblk), 1)
    n_adm = jnp.minimum(((past + qb * qblk + qcol) // CHUNK + 1) * CHUNK, nkeys)
    n_adm_max = jnp.minimum(((past + (qb + 1) * qblk - 1) // CHUNK + 1) * CHUNK, nkeys)
    nkc = (n_adm_max + kc - 1) // kc
    qit = qi_ref[...].T
    wt = (w_ref[...] * (IDX_HEADS ** -0.5)).T
    qip = [jnp.concatenate([qit[h * IDX_DIM:(h + 1) * IDX_DIM] for h in (2 * j, 2 * j + 1)], axis=1).astype(BF16)
           for j in range(IDX_HEADS // 2)]
    wh = [wt[IDX_DIM + h:IDX_DIM + h + 1] for h in range(IDX_HEADS)]
    kpos = lax.broadcasted_iota(I32, (kc, qblk), 0)

    def score_chunk(c, carry):
        mn, mx = carry
        base = pl.multiple_of(c * kc, kc)
        kic = kib_s[pl.ds(base, kc), :]
        sc = jnp.zeros((kc, qblk), F32)
        for j in range(IDX_HEADS // 2):
            rel = jnp.dot(kic, qip[j], preferred_element_type=F32)
            sc = (sc + wh[2 * j] * jnp.maximum(rel[:, :qblk], 0.0)
                  + wh[2 * j + 1] * jnp.maximum(rel[:, qblk:], 0.0))
        adm = (base + kpos) < n_adm
        sc_s[c] = jnp.where(adm, sc, NEG_INF)
        mn = jnp.minimum(mn, jnp.min(jnp.where(adm, sc, jnp.inf), axis=0, keepdims=True))
        mx = jnp.maximum(mx, jnp.max(jnp.where(adm, sc, -jnp.inf), axis=0, keepdims=True))
        return mn, mx

    mn, mx = lax.fori_loop(0, nkc, score_chunk,
                           (jnp.full((1, qblk), jnp.inf, F32), jnp.full((1, qblk), -jnp.inf, F32)))

    def count(pred):
        def body(c, acc):
            hit = jnp.where(pred(c, sc_s[c]), 1.0, 0.0)
            for i in range(kc // acc_rows):
                acc = acc + hit[i * acc_rows:(i + 1) * acc_rows]
            return acc
        acc = lax.fori_loop(0, nkc, body, jnp.zeros((acc_rows, qblk), F32))
        return jnp.sum(acc, axis=0, keepdims=True)

    def bracket_extremes(lo, hi):
        def body(c, carry):
            vmin, vmax = carry
            s = sc_s[c]
            a = jnp.where(s >= lo, s, jnp.inf)
            b = jnp.where(s < hi, s, -jnp.inf)
            for i in range(kc // acc_rows):
                vmin = jnp.minimum(vmin, a[i * acc_rows:(i + 1) * acc_rows])
                vmax = jnp.maximum(vmax, b[i * acc_rows:(i + 1) * acc_rows])
            return vmin, vmax
        vmin, vmax = lax.fori_loop(0, nkc, body, (jnp.full((acc_rows, qblk), jnp.inf, F32),
                                                  jnp.full((acc_rows, qblk), -jnp.inf, F32)))
        return jnp.min(vmin, axis=0, keepdims=True), jnp.max(vmax, axis=0, keepdims=True)

    cl0 = jnp.where(n_adm <= topk, kf, n_adm.astype(F32))
    hi0 = mx + (jnp.abs(mx) + jnp.abs(mn) + 1e-30) * 1e-6

    def mid_of(lo, hi):
        return lo + (hi - lo) * 0.5

    def bis_cond(cy):
        pending, it = cy[4], cy[5]
        return jnp.logical_and(it < 600, jnp.max(pending) > 0.0)

    def bis_pass(lo, hi, cl, ch):
        mid = mid_of(lo, hi)
        cnt = count(lambda c, s: s >= mid)
        inside = (mid > lo) & (mid < hi)
        up = inside & (cnt >= kf)
        dn = inside & (cnt < kf)
        return jnp.where(up, mid, lo), jnp.where(dn, mid, hi), jnp.where(up, cnt, cl), jnp.where(dn, cnt, ch)

    def bis_body(cy):
        lo, hi, cl, ch, _, it = cy
        for _ in range(BIS_PASSES_PER_CHECK):
            lo, hi, cl, ch = bis_pass(lo, hi, cl, ch)
        lo, top = bracket_extremes(lo, hi)
        mid = mid_of(lo, hi)
        done = (cl == kf) | (lo == top) | jnp.logical_not((mid > lo) & (mid < hi))
        return lo, hi, cl, ch, jnp.where(done, 0.0, 1.0), it + 1

    lo, hi, cl, ch, _, _ = lax.while_loop(
        bis_cond, bis_body,
        (mn, hi0, cl0, jnp.zeros((1, qblk), F32), jnp.where(cl0 == kf, 0.0, 1.0), jnp.int32(0)))

    need = kf - ch

    def tie_limit():
        def bit(i, j):
            jt = j + lax.shift_left(jnp.int32(1), jnp.asarray(jbits - 1 - i, I32))
            cnt = count(lambda c, s: (s >= lo) & (s < hi) & ((c * kc + kpos) < jt))
            return jnp.where(cnt <= need, jt, j)
        return lax.fori_loop(0, jbits, bit, jnp.zeros((1, qblk), I32))

    has_tie = jnp.max(jnp.where(cl > kf, 1.0, 0.0)) > 0.0
    jl = lax.cond(has_tie, tie_limit, lambda: jnp.full((1, qblk), 2 ** 30, I32))

    def mask_chunk(c, carry):
        s = sc_s[c]
        sel = ((s >= hi) | ((s >= lo) & ((c * kc + kpos) < jl))) & (s > 0.5 * NEG_INF)
        sc_s[c] = jnp.where(sel, 0.0, NEG_INF)
        return carry

    lax.fori_loop(0, nkc, mask_chunk, 0)

    qh_s[...] = (q_ref[...] * (DSA_HEAD_DIM ** -0.5)).T.astype(BF16)

    def logit_chunk(c, ms):
        base = pl.multiple_of(c * kc, kc)
        bias = sc_s[c]
        kch = kb_s[pl.ds(base, kc), :]
        out = []
        for h in range(DSA_HEADS):
            n = h // DSA_GROUP
            lg = jnp.dot(kch[:, n * DSA_HEAD_DIM:(n + 1) * DSA_HEAD_DIM],
                         qh_s[h * DSA_HEAD_DIM:(h + 1) * DSA_HEAD_DIM, :], preferred_element_type=F32) + bias
            lg_s[c, h] = lg
            out.append(jnp.maximum(ms[h], jnp.max(lg, axis=0, keepdims=True)))
        return tuple(out)

    ms = lax.fori_loop(0, nkc, logit_chunk, tuple(jnp.full((1, qblk), NEG_INF, F32) for _ in range(DSA_HEADS)))

    def value_chunk(c, carry):
        ls, accs = carry
        ls_new, accs_new = [], []
        for h in range(DSA_HEADS):
            n = h // DSA_GROUP
            p = jnp.exp(lg_s[c, h] - ms[h])
            ls_new.append(ls[h] + jnp.sum(p, axis=0, keepdims=True))
            accs_new.append(accs[h] + jnp.dot(vt_s[c, n * DSA_HEAD_DIM:(n + 1) * DSA_HEAD_DIM, :], p.astype(BF16),
                                              preferred_element_type=F32))
        return tuple(ls_new), tuple(accs_new)

    init = (tuple(jnp.zeros((1, qblk), F32) for _ in range(DSA_HEADS)),
            tuple(jnp.zeros((DSA_HEAD_DIM, qblk), F32) for _ in range(DSA_HEADS)))
    ls, accs = lax.fori_loop(0, nkc, value_chunk, init)
    out = jnp.concatenate([accs[h] / ls[h] for h in range(DSA_HEADS)], axis=0)
    o_ref[...] = out.T


def _dsa(q, qi, kiw, new_keys, cache_keys, nb, t, topk):
    past = cache_keys[0].shape[1] if cache_keys else 0
    kc = DSA_KEY_CHUNK
    lp = ((past + t + kc - 1) // kc) * kc
    qblk = min(t, LANE)
    assert past % kc == 0 and kc >= topk and t % qblk == 0 and qblk % CHUNK == 0
    nq = t // qblk
    jbits = int(math.ceil(math.log2(lp))) + 1

    def qspec(width):
        return pl.BlockSpec((qblk, width), lambda b, i: (b * nq + i, 0))

    def kspec(a):
        return pl.BlockSpec((None,) + a.shape[1:], lambda b, i: (b, 0, 0))

    keys = (tuple(cache_keys) if cache_keys else ()) + tuple(new_keys)
    return pl.pallas_call(
        functools.partial(_dsa_kernel, past=past, nkeys=past + t, topk=topk, kc=kc, qblk=qblk, jbits=jbits),
        grid=(nb, nq),
        in_specs=[qspec(DSA_WIDTH), qspec(IDX_HEADS * IDX_DIM), qspec(LANE)] + [kspec(a) for a in keys],
        out_specs=qspec(DSA_WIDTH),
        out_shape=jax.ShapeDtypeStruct((nb * t, DSA_WIDTH), F32),
        scratch_shapes=[pltpu.VMEM((lp, LANE), BF16), pltpu.VMEM((lp // kc, LANE, kc), BF16),
                        pltpu.VMEM((lp, IDX_DIM), BF16), pltpu.VMEM((lp // kc, kc, qblk), F32),
                        pltpu.VMEM((DSA_WIDTH, qblk), BF16), pltpu.VMEM((lp // kc, DSA_HEADS, kc, qblk), F32)],
        compiler_params=_cparams(("parallel", "arbitrary")),
        name="dsa",
    )(q, qi, kiw, *keys)


def _mem_kernel(q_ref, mk_ref, mv_ref, o_ref):
    scale = MEM_HEAD_DIM ** -0.5
    for h in range(MEM_HEADS):
        sl = slice(h * MEM_HEAD_DIM, (h + 1) * MEM_HEAD_DIM)
        qh = q_ref[:, sl].astype(BF16)
        kh = mk_ref[:, sl].astype(BF16)
        vh = mv_ref[:, sl].astype(BF16)
        lg = lax.dot_general(qh, kh, NT_DIMS, preferred_element_type=F32) * scale
        p = jnp.exp(lg - jnp.max(lg, axis=-1, keepdims=True))
        o = jnp.dot(p.astype(BF16), vh, preferred_element_type=F32)
        o_ref[:, sl] = o / jnp.sum(p, axis=-1, keepdims=True)


def _mem_attn(h, mk, mv, kcol, vcol, row0, nb, t):
    tq = _tile(t, 256)
    nt = t // tq
    rb0 = row0 // tq
    m = mk.shape[1]
    return pl.pallas_call(
        _mem_kernel,
        grid=(nb, nt),
        in_specs=[pl.BlockSpec((tq, MEM_WIDTH), lambda b, i: (rb0 + b * nt + i, C_MQ // MEM_WIDTH)),
                  pl.BlockSpec((None, m, MEM_WIDTH), lambda b, i: (b, 0, kcol)),
                  pl.BlockSpec((None, m, MEM_WIDTH), lambda b, i: (b, 0, vcol))],
        out_specs=pl.BlockSpec((tq, MEM_WIDTH), lambda b, i: (b * nt + i, 0)),
        out_shape=jax.ShapeDtypeStruct((nb * t, MEM_WIDTH), F32),
        compiler_params=_cparams(("parallel", "parallel")),
        name="mem_attn",
    )(h, mk, mv)


def _merge_kernel(*refs, alpha, first_tiles):
    br_a, br_b = refs[0:N_BRANCH], refs[N_BRANCH:2 * N_BRANCH]
    g_refs = refs[2 * N_BRANCH:3 * N_BRANCH]
    x_ref, wb_ref, wo_ref, lg_ref, lb_ref, o_ref = refs[3 * N_BRANCH:]
    in_first = pl.program_id(0) < first_tiles
    merged = None
    for n in range(N_BRANCH):
        br = jnp.where(in_first, br_a[n][...], br_b[n][...])
        proj = jnp.dot(br.astype(BF16), wb_ref[n], preferred_element_type=F32)
        term = _sigmoid(g_refs[n][...]) * proj
        merged = term if merged is None else merged + term
    m = jnp.dot(merged.astype(BF16), wo_ref[...], preferred_element_type=F32)
    o_ref[...] = _layer_norm(alpha * x_ref[...] + m, lg_ref[...], lb_ref[...])


def _merge(branches_a, branches_b, h, x, wb, wo, lng, lnb, alpha):
    n = x.shape[0]
    rows_a, rows_b = branches_a[0].shape[0], branches_b[0].shape[0]
    tm = _tile(math.gcd(rows_a, rows_b), 256)
    ta, tb = rows_a // tm, rows_b // tm
    assert rows_a + rows_b == n
    aspec = pl.BlockSpec((tm, BRANCH_WIDTH), lambda i: (jnp.minimum(i, ta - 1), 0))
    bspec = pl.BlockSpec((tm, BRANCH_WIDTH), lambda i: (jnp.maximum(i - ta, 0), 0))

    def gspec(j):
        return pl.BlockSpec((tm, D_MODEL), lambda i: (i, C_GATE // D_MODEL + j))

    xspec = pl.BlockSpec((tm, D_MODEL), lambda i: (i, 0))
    vspec = pl.BlockSpec((1, D_MODEL), lambda i: (0, 0))
    return pl.pallas_call(
        functools.partial(_merge_kernel, alpha=alpha, first_tiles=ta),
        grid=(ta + tb,),
        in_specs=[aspec] * N_BRANCH + [bspec] * N_BRANCH + [gspec(0), gspec(1), gspec(2), xspec,
                  pl.BlockSpec((N_BRANCH, BRANCH_WIDTH, D_MODEL), lambda i: (0, 0, 0)),
                  pl.BlockSpec((D_MODEL, D_MODEL), lambda i: (0, 0)), vspec, vspec],
        out_specs=xspec,
        out_shape=jax.ShapeDtypeStruct((n, D_MODEL), F32),
        compiler_params=_cparams(("arbitrary",)),
        name="merge",
    )(*branches_a, *branches_b, h, h, h, x, wb, wo, lng, lnb)


def _swiglu_partial(xb, wg_ref, wu_ref, wd_ref):
    g = jnp.dot(xb, wg_ref[...], preferred_element_type=F32)
    u = jnp.dot(xb, wu_ref[...], preferred_element_type=F32)
    a = (g * _sigmoid(g)) * u
    return jnp.dot(a.astype(BF16), wd_ref[...], preferred_element_type=F32)


def _ffn_kernel(x_ref, wg_ref, wu_ref, wd_ref, lg_ref, lb_ref, o_ref, xb_s, acc_s, *, alpha):
    k = pl.program_id(1)

    @pl.when(k == 0)
    def _():
        xb_s[...] = x_ref[...].astype(BF16)
        acc_s[...] = jnp.zeros_like(acc_s)

    acc_s[...] += _swiglu_partial(xb_s[...], wg_ref, wu_ref, wd_ref)

    @pl.when(k == pl.num_programs(1) - 1)
    def _():
        o_ref[...] = _layer_norm(alpha * x_ref[...] + acc_s[...], lg_ref[...], lb_ref[...])


def _ffn_dense(x, w_gu, w_down, lng, lnb, alpha):
    n = x.shape[0]
    f = w_down.shape[0]
    tm = _tile(n, 512)
    fc = _tile(f, 1536, LANE)
    nk = f // fc
    xspec = pl.BlockSpec((tm, D_MODEL), lambda i, k: (i, 0))
    vspec = pl.BlockSpec((1, D_MODEL), lambda i, k: (0, 0))
    return pl.pallas_call(
        functools.partial(_ffn_kernel, alpha=alpha),
        grid=(n // tm, nk),
        in_specs=[xspec,
                  pl.BlockSpec((D_MODEL, fc), lambda i, k: (0, k)),
                  pl.BlockSpec((D_MODEL, fc), lambda i, k: (0, nk + k)),
                  pl.BlockSpec((fc, D_MODEL), lambda i, k: (k, 0)), vspec, vspec],
        out_specs=xspec,
        out_shape=jax.ShapeDtypeStruct((n, D_MODEL), F32),
        scratch_shapes=[pltpu.VMEM((tm, D_MODEL), BF16), pltpu.VMEM((tm, D_MODEL), F32)],
        compiler_params=_cparams(("parallel", "arbitrary")),
        name="ffn_dense",
    )(x, w_gu, w_gu, w_down, lng, lnb)


def _router_kernel(x_ref, w_ref, e_ref, g_ref):
    x = x_ref[...]
    w = w_ref[...]
    xh = x.astype(BF16)
    xl = (x - xh.astype(F32)).astype(BF16)
    wh = w.astype(BF16)
    wl = (w - wh.astype(F32)).astype(BF16)
    lg = (jnp.dot(xh, wh, preferred_element_type=F32) + jnp.dot(xh, wl, preferred_element_type=F32)
          + jnp.dot(xl, wh, preferred_element_type=F32))
    lane = lax.broadcasted_iota(I32, lg.shape, 1).astype(F32)
    lg = jnp.where(lane < N_EXPERTS, lg, -jnp.inf)
    m1 = jnp.max(lg, axis=-1, keepdims=True)
    i1 = jnp.min(jnp.where(lg == m1, lane, float(LANE)), axis=-1, keepdims=True)
    lg2 = jnp.where(lane == i1, -jnp.inf, lg)
    m2 = jnp.max(lg2, axis=-1, keepdims=True)
    i2 = jnp.min(jnp.where(lg2 == m2, lane, float(LANE)), axis=-1, keepdims=True)
    e = jnp.exp(m2 - m1)
    g1 = 1.0 / (1.0 + e)
    g2 = e / (1.0 + e)
    e_ref[...] = jnp.where(lane == 0.0, i1, jnp.where(lane == 1.0, i2, 0.0)).astype(I32)
    g_ref[...] = jnp.where(lane == 0.0, g1, jnp.where(lane == 1.0, g2, 0.0))


def _router(x, w_router):
    n = x.shape[0]
    tm = _tile(n, 512)
    wp = jnp.pad(w_router, ((0, 0), (0, LANE - N_EXPERTS)))
    ospec = pl.BlockSpec((tm, LANE), lambda i: (i, 0))
    return pl.pallas_call(
        _router_kernel,
        grid=(n // tm,),
        in_specs=[pl.BlockSpec((tm, D_MODEL), lambda i: (i, 0)), pl.BlockSpec((D_MODEL, LANE), lambda i: (0, 0))],
        out_specs=[ospec, ospec],
        out_shape=[jax.ShapeDtypeStruct((n, LANE), I32), jax.ShapeDtypeStruct((n, LANE), F32)],
        compiler_params=_cparams(("parallel",)),
        name="router",
    )(x, wp)


def _row_copy(src, dst, s, d, sem):
    return pltpu.make_async_copy(src.at[pl.ds(s, 1)], dst.at[pl.ds(d, 1)], sem)


def _dispatch_kernel(pos_ref, x_ref, xs_in, xs_hbm, sem, *, tb):
    del xs_in
    i = pl.program_id(0)

    def start(r, c):
        t = i * tb + r
        for j in range(TOP_K):
            _row_copy(x_ref, xs_hbm, r, pos_ref[TOP_K * t + j], sem).start()
        return c

    lax.fori_loop(0, tb, start, 0)
    for j in range(TOP_K):
        pltpu.make_async_copy(x_ref, xs_hbm.at[pl.ds(0, tb)], sem).wait()


def _dispatch(pos, x, n_rows):
    n = x.shape[0]
    tb = _tile(n, 256)
    xs0 = jnp.zeros((n_rows, D_MODEL), F32)
    return pl.pallas_call(
        functools.partial(_dispatch_kernel, tb=tb),
        grid_spec=pltpu.PrefetchScalarGridSpec(
            num_scalar_prefetch=1, grid=(n // tb,),
            in_specs=[pl.BlockSpec((tb, D_MODEL), lambda i, p: (i, 0)), pl.BlockSpec(memory_space=pl.ANY)],
            out_specs=pl.BlockSpec(memory_space=pl.ANY),
            scratch_shapes=[pltpu.SemaphoreType.DMA(())]),
        out_shape=jax.ShapeDtypeStruct((n_rows, D_MODEL), F32),
        input_output_aliases={2: 0},
        compiler_params=pltpu.CompilerParams(dimension_semantics=("arbitrary",), has_side_effects=True),
        name="moe_dispatch",
    )(pos, x, xs0)


def _expert_kernel(te_ref, nu_ref, x_ref, wg_ref, wu_ref, wd_ref, o_ref, xb_s, acc_s):
    del te_ref
    i = pl.program_id(0)
    k = pl.program_id(1)
    used = i < nu_ref[0]

    @pl.when(jnp.logical_and(used, k == 0))
    def _():
        xb_s[...] = x_ref[...].astype(BF16)
        acc_s[...] = jnp.zeros_like(acc_s)

    @pl.when(used)
    def _():
        acc_s[...] += _swiglu_partial(xb_s[...], wg_ref, wu_ref, wd_ref)

    @pl.when(k == pl.num_programs(1) - 1)
    def _():
        o_ref[...] = jnp.where(used, acc_s[...], 0.0)


def _experts(tile_expert, n_used, xs, w_gu, w_down, tm):
    n_rows = xs.shape[0]
    f = w_down.shape[1]
    fc = _tile(f, 1024, LANE)
    nk = f // fc
    xspec = pl.BlockSpec((tm, D_MODEL), lambda i, k, te, nu: (i, 0))
    return pl.pallas_call(
        _expert_kernel,
        grid_spec=pltpu.PrefetchScalarGridSpec(
            num_scalar_prefetch=2, grid=(n_rows // tm, nk),
            in_specs=[xspec,
                      pl.BlockSpec((None, D_MODEL, fc), lambda i, k, te, nu: (te[i], 0, k)),
                      pl.BlockSpec((None, D_MODEL, fc), lambda i, k, te, nu: (te[i], 0, nk + k)),
                      pl.BlockSpec((None, fc, D_MODEL), lambda i, k, te, nu: (te[i], k, 0))],
            out_specs=xspec,
            scratch_shapes=[pltpu.VMEM((tm, D_MODEL), BF16), pltpu.VMEM((tm, D_MODEL), F32)]),
        out_shape=jax.ShapeDtypeStruct((n_rows, D_MODEL), F32),
        compiler_params=_cparams(("arbitrary", "arbitrary")),
        name="moe_experts",
    )(tile_expert, n_used, xs, w_gu, w_gu, w_down)


def _combine_kernel(pos_ref, ys_hbm, g_ref, x_ref, lg_ref, lb_ref, o_ref, buf, sem, *, tb, alpha):
    i = pl.program_id(0)

    def start(r, c):
        t = i * tb + r
        for j in range(TOP_K):
            _row_copy(ys_hbm, buf.at[j], pos_ref[TOP_K * t + j], r, sem.at[j]).start()
        return c

    lax.fori_loop(0, tb, start, 0)
    for j in range(TOP_K):
        pltpu.make_async_copy(ys_hbm.at[pl.ds(0, tb)], buf.at[j], sem.at[j]).wait()
    g = g_ref[...]
    f = g[:, 0:1] * buf[0] + g[:, 1:2] * buf[1]
    o_ref[...] = _layer_norm(alpha * x_ref[...] + f, lg_ref[...], lb_ref[...])


def _combine(pos, ys, gates, x, lng, lnb, alpha):
    n = x.shape[0]
    tb = _tile(n, 256)
    xspec = pl.BlockSpec((tb, D_MODEL), lambda i, p: (i, 0))
    vspec = pl.BlockSpec((1, D_MODEL), lambda i, p: (0, 0))
    return pl.pallas_call(
        functools.partial(_combine_kernel, tb=tb, alpha=alpha),
        grid_spec=pltpu.PrefetchScalarGridSpec(
            num_scalar_prefetch=1, grid=(n // tb,),
            in_specs=[pl.BlockSpec(memory_space=pl.ANY), pl.BlockSpec((tb, LANE), lambda i, p: (i, 0)),
                      xspec, vspec, vspec],
            out_specs=xspec,
            scratch_shapes=[pltpu.VMEM((TOP_K, tb, D_MODEL), F32), pltpu.SemaphoreType.DMA((TOP_K,))]),
        out_shape=jax.ShapeDtypeStruct((n, D_MODEL), F32),
        compiler_params=_cparams(("arbitrary",)),
        name="moe_combine",
    )(pos, ys, gates, x, lng, lnb)


def _moe(x, w_router, w_gu, w_down, lng, lnb, alpha):
    n = x.shape[0]
    tm = _tile(n, 512)
    eidx, gates = _router(x, w_router)
    e2 = eidx[:, :TOP_K].reshape(-1)
    onehot = (e2[:, None] == jnp.arange(N_EXPERTS, dtype=I32)[None, :]).astype(I32)
    csum = jnp.cumsum(onehot, axis=0)
    rank = jnp.take_along_axis(csum, e2[:, None], axis=1)[:, 0] - 1
    counts = csum[-1]
    padded = ((counts + tm - 1) // tm) * tm
    ends = jnp.cumsum(padded)
    pos = ((ends - padded)[e2] + rank).astype(I32)
    n_rows = TOP_K * n + N_EXPERTS * tm
    n_tiles = n_rows // tm
    tile_start = jnp.arange(n_tiles, dtype=I32) * tm
    tile_expert = jnp.minimum(jnp.sum((tile_start[:, None] >= ends[None, :]).astype(I32), axis=1),
                              N_EXPERTS - 1).astype(I32)
    n_used = (ends[-1:] // tm).astype(I32)
    xs = _dispatch(pos, x, n_rows)
    ys = _experts(tile_expert, n_used, xs, w_gu, w_down, tm)
    return _combine(pos, ys, gates, x, lng, lnb, alpha)


def _pack_w_in(w):
    d = w.shape[0]
    parts, start = [], 0
    for n in IN_SIZES:
        parts.append(w[:, start:start + n])
        start += n
    hq, hf, hi, hg, dq, dk, dv, iq, ik, iw, mq, gates = parts
    ikw = jnp.concatenate([ik, iw, jnp.zeros((d, LANE - IDX_DIM - IDX_HEADS), w.dtype)], axis=1)
    tail = jnp.zeros((d, H_COLS - (C_IK + LANE)), w.dtype)
    return jnp.concatenate([hq, hf, hi, hg, gates, dq, mq, iq, dk, dv, ikw, tail], axis=1).astype(BF16)


def kernel(x_prompt, x_sample, cache_dsa_k, cache_dsa_v, cache_idx_k, state_hgrn, cache_mem_k, cache_mem_v,
           mem_prompt, w_in, hgrn_lb_logits, hgrn_norm_g, w_branch, w_out, w_mem_kv, ln_g, ln_b, ffn_w_gate_up,
           ffn_w_down, moe_router, moe_w_gate_up, moe_w_down):
    depth = w_in.shape[0]
    alpha = (2 * depth) ** 0.25
    bp, tp, d = x_prompt.shape
    bs, ts, _ = x_sample.shape
    past = cache_dsa_k.shape[2]
    m = mem_prompt.shape[1]
    np_rows, ns_rows = bp * tp, bs * ts

    sm = jax.nn.softmax(hgrn_lb_logits.astype(F32), axis=0)
    lbs = jnp.cumsum(sm, axis=0) - sm[0:1]

    x = jnp.concatenate([x_prompt.reshape(np_rows, d), x_sample.reshape(ns_rows, d)], axis=0)
    n = x.shape[0]
    tables_p = _rope_tables(jnp.arange(tp, dtype=I32))
    tables_s = _rope_tables(past + jnp.arange(ts, dtype=I32))
    mem_flat = mem_prompt.reshape(bp * m, d)
    zeros_state = jnp.zeros((bp, HG_HEADS, HG_DK, HG_DV), F32)
    outs = {k: [] for k in ("pk", "pv", "pki", "ph", "pmk", "pmv", "sk", "sv", "ski", "sh")}
    for l in range(depth):
        h = _matmul(x, _pack_w_in(w_in[l]), _tile(n, 512), IN_TN, "in_proj")
        mem_kv = _matmul(mem_flat, w_mem_kv[l].astype(BF16), _tile(bp * m, 512), 512, "mem_kv").reshape(bp, m, 2 * MEM_WIDTH)
        lb = lbs[l].reshape(1, HG_WIDTH)
        ng = hgrn_norm_g[l].reshape(1, HG_WIDTH).astype(F32)

        qp, kp, qip, kip = _rope(h, tables_p, 0, bp, tp)
        vp = h[:np_rows, C_DV:C_DV + LANE]
        kip32 = kip[:, :IDX_DIM]
        ohg_p, sh_p = _hgrn(h, lb, ng, zeros_state, 0, bp, tp)
        odsa_p = _dsa(qp, qip, kip,
                      (kp.reshape(bp, tp, LANE), vp.reshape(bp, tp, LANE), kip32.reshape(bp, tp, IDX_DIM)), None,
                      bp, tp, min(TOPK_MAX, tp // 4))
        omem_p = _mem_attn(h, mem_kv, mem_kv, 0, 1, 0, bp, tp)

        qs, ks, qis, kis = _rope(h, tables_s, np_rows, bs, ts)
        vs = h[np_rows:, C_DV:C_DV + LANE]
        kis32 = kis[:, :IDX_DIM]
        ohg_s, sh_s = _hgrn(h, lb, ng, state_hgrn[l], np_rows, bs, ts)
        odsa_s = _dsa(qs, qis, kis,
                      (ks.reshape(bs, ts, LANE), vs.reshape(bs, ts, LANE), kis32.reshape(bs, ts, IDX_DIM)),
                      (cache_dsa_k[l].reshape(bs, past, LANE), cache_dsa_v[l].reshape(bs, past, LANE), cache_idx_k[l]),
                      bs, ts, min(TOPK_MAX, (past + ts) // 4))
        omem_s = _mem_attn(h, cache_mem_k[l].reshape(bs, m, MEM_WIDTH), cache_mem_v[l].reshape(bs, m, MEM_WIDTH),
                           0, 0, np_rows, bs, ts)

        x = _merge((ohg_p, odsa_p, omem_p), (ohg_s, odsa_s, omem_s), h, x, w_branch[l].astype(BF16),
                   w_out[l].astype(BF16), ln_g[l, 0].reshape(1, d), ln_b[l, 0].reshape(1, d), alpha)
        if l % 2 == 0:
            x = _ffn_dense(x, ffn_w_gate_up[l // 2].astype(BF16), ffn_w_down[l // 2].astype(BF16),
                           ln_g[l, 1].reshape(1, d), ln_b[l, 1].reshape(1, d), alpha)
        else:
            x = _moe(x, moe_router[l // 2], moe_w_gate_up[l // 2].astype(BF16), moe_w_down[l // 2].astype(BF16),
                     ln_g[l, 1].reshape(1, d), ln_b[l, 1].reshape(1, d), alpha)

        outs["pk"].append(kp.reshape(bp, tp, DSA_KV_HEADS, DSA_HEAD_DIM))
        outs["pv"].append(vp.reshape(bp, tp, DSA_KV_HEADS, DSA_HEAD_DIM))
        outs["pki"].append(kip32.reshape(bp, tp, IDX_DIM))
        outs["ph"].append(sh_p)
        outs["pmk"].append(mem_kv[..., :MEM_WIDTH].reshape(bp, m, MEM_HEADS, MEM_HEAD_DIM))
        outs["pmv"].append(mem_kv[..., MEM_WIDTH:].reshape(bp, m, MEM_HEADS, MEM_HEAD_DIM))
        outs["sk"].append(ks.reshape(bs, ts, DSA_KV_HEADS, DSA_HEAD_DIM))
        outs["sv"].append(vs.reshape(bs, ts, DSA_KV_HEADS, DSA_HEAD_DIM))
        outs["ski"].append(kis32.reshape(bs, ts, IDX_DIM))
        outs["sh"].append(sh_s)

    st = {k: jnp.stack(v) for k, v in outs.items()}
    y_prompt = x[:np_rows].reshape(bp, tp, d)
    y_sample = x[np_rows:].reshape(bs, ts, d)
    return (y_prompt, y_sample, st["pk"], st["pv"], st["pki"], st["ph"], st["pmk"], st["pmv"],
            st["sk"], st["sv"], st["ski"], st["sh"])
```

```python
import functools
import math

import jax
import jax.numpy as jnp
from jax import lax
from jax.experimental import pallas as pl
from jax.experimental.pallas import tpu as pltpu

F32, BF16, I32 = jnp.float32, jnp.bfloat16, jnp.int32

D_MODEL = 1024
CHUNK = 64
HG_HEADS, HG_DK, HG_DV, HG_BLOCK = 4, 128, 128, 32
HG_WIDTH = HG_HEADS * HG_DV
LB_FLOOR = 1e-30
DSA_HEADS, DSA_KV_HEADS, DSA_HEAD_DIM = 8, 2, 64
DSA_GROUP = DSA_HEADS // DSA_KV_HEADS
DSA_WIDTH = DSA_HEADS * DSA_HEAD_DIM
IDX_HEADS, IDX_DIM = 8, 32
TOPK_MAX = 256
Q_BLOCK = 64
MEM_HEADS, MEM_HEAD_DIM = 4, 128
MEM_WIDTH = MEM_HEADS * MEM_HEAD_DIM
N_BRANCH, BRANCH_WIDTH = 3, 512
N_EXPERTS, TOP_K = 8, 2
ROPE_THETA = 10000.0
LN_EPS = 1e-5
RMS_EPS = 1e-6
NEG_INF = -1e30
IN_SIZES = (HG_HEADS * HG_DK, HG_HEADS * HG_DK, HG_HEADS * HG_DV, HG_WIDTH,
            DSA_WIDTH, DSA_KV_HEADS * DSA_HEAD_DIM, DSA_KV_HEADS * DSA_HEAD_DIM,
            IDX_HEADS * IDX_DIM, IDX_DIM, IDX_HEADS, MEM_WIDTH, N_BRANCH * D_MODEL)

LANE = 128
SUBLANE = 8
VMEM_LIMIT = 48 * 1024 * 1024

C_HQ, C_HF, C_HI, C_HG = 0, 512, 1024, 1536
C_GATE = 2048
C_DQ = 5120
C_MQ = 5632
C_IQ = 6144
C_DK = 6400
C_DV = 6528
C_IK = 6656
H_COLS = 6912
IN_TN = 768
IN_TM = 1024

DSA_KEY_CHUNK = 256
BIS_PASSES_PER_CHECK = 4

NT_DIMS = (((1,), (1,)), ((), ()))


def _cparams(sem):
    return pltpu.CompilerParams(dimension_semantics=sem, vmem_limit_bytes=VMEM_LIMIT)


def _tile(n, pref, mult=SUBLANE):
    t = min(n, pref)
    while n % t or t % mult:
        t -= 1
    return t


def _sigmoid(x):
    return 1.0 / (1.0 + jnp.exp(-x))


def _layer_norm(v, g, b):
    mu = jnp.mean(v, axis=-1, keepdims=True)
    d = v - mu
    var = jnp.mean(d * d, axis=-1, keepdims=True)
    return d * lax.rsqrt(var + LN_EPS) * g + b


def _mm_kernel(x_ref, w_ref, o_ref):
    o_ref[...] = jnp.dot(x_ref[...].astype(BF16), w_ref[...], preferred_element_type=F32)


def _matmul(x, w, tm, tn, name):
    n, k = x.shape
    nc = w.shape[1]
    return pl.pallas_call(
        _mm_kernel,
        grid=(n // tm, nc // tn),
        in_specs=[pl.BlockSpec((tm, k), lambda i, j: (i, 0)),
                  pl.BlockSpec((k, tn), lambda i, j: (0, j))],
        out_specs=pl.BlockSpec((tm, tn), lambda i, j: (i, j)),
        out_shape=jax.ShapeDtypeStruct((n, nc), F32),
        compiler_params=_cparams(("parallel", "arbitrary")),
        name=name,
    )(x, w)


def _rot_half(x, half):
    lane = lax.broadcasted_iota(I32, x.shape, 1)
    lo = (lane % (2 * half)) < half
    return jnp.where(lo, pltpu.roll(x, LANE - half, 1), pltpu.roll(x, half, 1))


def _rope_kernel(dq_ref, dk_ref, dv_ref, iq_ref, ik_ref, c64_ref, s64_ref, c32_ref, s32_ref, cik_ref, sik_ref,
                 q_ref, k_ref, v_ref, qi_ref, ki_ref):
    v_ref[...] = dv_ref[...]
    c64, s64 = c64_ref[...], s64_ref[...]
    c32, s32 = c32_ref[...], s32_ref[...]
    for j in range(DSA_WIDTH // LANE):
        x = dq_ref[:, j * LANE:(j + 1) * LANE]
        q_ref[:, j * LANE:(j + 1) * LANE] = x * c64 + _rot_half(x, DSA_HEAD_DIM // 2) * s64
    x = dk_ref[...]
    k_ref[...] = x * c64 + _rot_half(x, DSA_HEAD_DIM // 2) * s64
    for j in range(IDX_HEADS * IDX_DIM // LANE):
        x = iq_ref[:, j * LANE:(j + 1) * LANE]
        qi_ref[:, j * LANE:(j + 1) * LANE] = x * c32 + _rot_half(x, IDX_DIM // 2) * s32
    x = ik_ref[...]
    ki_ref[...] = x * cik_ref[...] + _rot_half(x, IDX_DIM // 2) * sik_ref[...]


def _rope_tables(pos):
    lane = jnp.arange(LANE)
    p = pos.astype(F32)[:, None]
    h64 = DSA_HEAD_DIM // 2
    inv64 = ROPE_THETA ** (-jnp.arange(h64, dtype=F32) / h64)
    a64 = p * inv64[lane % h64][None, :]
    c64 = jnp.cos(a64)
    s64 = jnp.sin(a64) * jnp.where((lane % DSA_HEAD_DIM) < h64, -1.0, 1.0)[None, :]
    h32 = IDX_DIM // 2
    inv32 = ROPE_THETA ** (-jnp.arange(h32, dtype=F32) / h32)
    a32 = p * inv32[lane % h32][None, :]
    c32 = jnp.cos(a32)
    s32 = jnp.sin(a32) * jnp.where((lane % IDX_DIM) < h32, -1.0, 1.0)[None, :]
    live = (lane < IDX_DIM)[None, :]
    cik = jnp.where(live, c32, 1.0)
    sik = jnp.where(live, s32, 0.0)
    return tuple(t.astype(F32) for t in (c64, s64, c32, s32, cik, sik))


def _rope(h, tables, row0, nb, t):
    tr = _tile(t, 256)
    nt = t // tr
    rb0 = row0 // tr
    rows = nb * t

    def hspec(width, col):
        return pl.BlockSpec((tr, width), lambda i: (rb0 + i, col // width))

    tspec = pl.BlockSpec((tr, LANE), lambda i: (i % nt, 0))

    def ospec(width):
        return pl.BlockSpec((tr, width), lambda i: (i, 0))

    return pl.pallas_call(
        _rope_kernel,
        grid=(rows // tr,),
        in_specs=[hspec(DSA_WIDTH, C_DQ), hspec(LANE, C_DK), hspec(LANE, C_DV), hspec(IDX_HEADS * IDX_DIM, C_IQ),
                  hspec(LANE, C_IK)] + [tspec] * 6,
        out_specs=[ospec(DSA_WIDTH), ospec(LANE), ospec(LANE), ospec(IDX_HEADS * IDX_DIM), ospec(LANE)],
        out_shape=[jax.ShapeDtypeStruct((rows, DSA_WIDTH), F32), jax.ShapeDtypeStruct((rows, LANE), F32),
                   jax.ShapeDtypeStruct((rows, LANE), F32),
                   jax.ShapeDtypeStruct((rows, IDX_HEADS * IDX_DIM), F32), jax.ShapeDtypeStruct((rows, LANE), F32)],
        compiler_params=_cparams(("parallel",)),
        name="rope",
    )(h, h, h, h, h, *tables)


def _hgrn_kernel(q_ref, f_ref, i_ref, g_ref, lb_ref, ng_ref, s0_ref, o_ref, s_ref,
                 st_s, b_s, k_s, v_s, q_s, o_s, *, tt):
    t = pl.program_id(1)
    c = HG_BLOCK

    @pl.when(t == 0)
    def _():
        for hd in range(HG_HEADS):
            st_s[hd] = s0_ref[hd].T

    r = lax.broadcasted_iota(I32, (tt, tt), 0)
    cc = lax.broadcasted_iota(I32, (tt, tt), 1)
    tri = jnp.where(((r // c) == (cc // c)) & (cc <= r), 1.0, 0.0).astype(BF16)
    for hd in range(HG_HEADS):
        hl = slice(hd * LANE, (hd + 1) * LANE)
        lb = lb_ref[:, hl]
        z = f_ref[:, hl]
        lf = jnp.log(jnp.maximum(lb, LB_FLOOR) + (1.0 - lb) * _sigmoid(z))
        k_s[hd] = (1.0 - lb) * _sigmoid(-z)
        hq = q_ref[:, hl]
        q_s[hd] = hq * _sigmoid(hq)
        v_s[hd] = i_ref[:, hl]
        hi = lf.astype(BF16)
        r1 = lf - hi.astype(F32)
        mid = r1.astype(BF16)
        lo = (r1 - mid.astype(F32)).astype(BF16)
        b_s[hd] = (jnp.dot(tri, hi, preferred_element_type=F32) + jnp.dot(tri, mid, preferred_element_type=F32)
                   + jnp.dot(tri, lo, preferred_element_type=F32))

    row8 = lax.broadcasted_iota(I32, (SUBLANE, LANE), 0)
    nsub = c // SUBLANE

    def block(j, carry):
        base = pl.multiple_of(j * c, c)
        for hd in range(HG_HEADS):
            hl = slice(hd * LANE, (hd + 1) * LANE)
            bj = b_s[hd, pl.ds(base, c), :]
            qj = q_s[hd, pl.ds(base, c), :]
            kj = k_s[hd, pl.ds(base, c), :]
            vj = v_s[hd, pl.ds(base, c), :]
            blast = b_s[hd, pl.ds(base + c - 1, 1), :]
            oc = [jnp.zeros((SUBLANE, LANE), F32) for _ in range(nsub)]
            for s in range(c):
                bs = b_s[hd, pl.ds(base + s, 1), :]
                ks = k_s[hd, pl.ds(base + s, 1), :]
                vs = v_s[hd, pl.ds(base + s, 1), :]
                for u in range(s // SUBLANE, nsub):
                    d = bj[u * SUBLANE:(u + 1) * SUBLANE] - bs
                    if u == s // SUBLANE:
                        d = jnp.where(row8 >= (s % SUBLANE), d, NEG_INF)
                    a = jnp.sum(qj[u * SUBLANE:(u + 1) * SUBLANE] * jnp.exp(d) * ks, axis=-1, keepdims=True)
                    oc[u] = oc[u] + a * vs
            o_intra = jnp.concatenate(oc, axis=0)
            st = st_s[hd]
            qe = qj * jnp.exp(bj)
            o_inter = lax.dot_general(qe.astype(BF16), st.astype(BF16), NT_DIMS, preferred_element_type=F32)
            ke = kj * jnp.exp(blast - bj)
            upd = jnp.dot(vj.T.astype(BF16), ke.astype(BF16), preferred_element_type=F32)
            st_s[hd] = jnp.exp(blast) * st + upd
            o_s[pl.ds(base, c), hl] = o_intra + o_inter
        return carry

    lax.fori_loop(0, tt // c, block, 0)

    for hd in range(HG_HEADS):
        hl = slice(hd * LANE, (hd + 1) * LANE)
        o = o_s[:, hl]
        o = o * lax.rsqrt(jnp.mean(o * o, axis=-1, keepdims=True) + RMS_EPS)
        o_ref[:, hl] = o * ng_ref[:, hl] * _sigmoid(g_ref[:, hl])

    @pl.when(t == pl.num_programs(1) - 1)
    def _():
        for hd in range(HG_HEADS):
            s_ref[hd] = st_s[hd].T


def _hgrn(h, lb, ng, s0, row0, nb, t):
    assert t % HG_BLOCK == 0
    tt = _tile(t, 256, HG_BLOCK)
    nt = t // tt
    rb0 = row0 // tt

    def hspec(col):
        return pl.BlockSpec((tt, HG_WIDTH), lambda b, i: (rb0 + b * nt + i, col // HG_WIDTH))

    vspec = pl.BlockSpec((1, HG_WIDTH), lambda b, i: (0, 0))
    sspec = pl.BlockSpec((None, HG_HEADS, HG_DK, HG_DV), lambda b, i: (b, 0, 0, 0))
    return pl.pallas_call(
        functools.partial(_hgrn_kernel, tt=tt),
        grid=(nb, nt),
        in_specs=[hspec(C_HQ), hspec(C_HF), hspec(C_HI), hspec(C_HG), vspec, vspec, sspec],
        out_specs=[pl.BlockSpec((tt, HG_WIDTH), lambda b, i: (b * nt + i, 0)), sspec],
        out_shape=[jax.ShapeDtypeStruct((nb * t, HG_WIDTH), F32),
                   jax.ShapeDtypeStruct((nb, HG_HEADS, HG_DK, HG_DV), F32)],
        scratch_shapes=[pltpu.VMEM((HG_HEADS, HG_DV, HG_DK), F32)] + [pltpu.VMEM((HG_HEADS, tt, LANE), F32)] * 4
        + [pltpu.VMEM((tt, HG_WIDTH), F32)],
        compiler_params=_cparams(("parallel", "arbitrary")),
        name="hgrn2",
    )(h, h, h, h, lb, ng, s0)


def _dsa_kernel(q_ref, qi_ref, w_ref, *refs, past, nkeys, topk, kc, qblk, jbits):
    n_in = 6 if past else 3
    key_refs, (o_ref, kb_s, vt_s, kib_s, sc_s, qh_s, lg_s) = refs[:n_in], refs[n_in:]
    qb = pl.program_id(1)
    kf = float(topk)
    acc_rows = 4 * SUBLANE
    lp = kb_s.shape[0]
    segs = [(past, nkeys - past) + tuple(key_refs[-3:])]
    if past:
        segs.insert(0, (0, past) + tuple(key_refs[:3]))

    @pl.when(qb == 0)
    def _():
        for s0, n, k_ref, _, ki_ref in segs:
            kb_s[s0:s0 + n, :] = k_ref[...].astype(BF16)
            kib_s[s0:s0 + n, :] = ki_ref[...].astype(BF16)
        if lp > nkeys:
            kb_s[nkeys:, :] = jnp.zeros((lp - nkeys, LANE), BF16)
            kib_s[nkeys:, :] = jnp.zeros((lp - nkeys, IDX_DIM), BF16)
        for c in range(lp // kc):
            pieces = []
            for s0, n, _, v_ref, _ in segs:
                a, b = max(c * kc, s0), min((c + 1) * kc, s0 + n)
                if a < b:
                    pieces.append(v_ref[a - s0:b - s0, :])
            have = sum(p.shape[0] for p in pieces)
            if have < kc:
                pieces.append(jnp.zeros((kc - have, LANE), F32))
            vt_s[c] = (pieces[0] if len(pieces) == 1 else jnp.concatenate(pieces, axis=0)).T.astype(BF16)

    qcol = lax.broadcasted_iota(I32, (1, qblk), 1)
    n_adm = jnp.minimum(((past + qb * qblk + qcol) // CHUNK + 1) * CHUNK, nkeys)
    n_adm_max = jnp.minimum(((past + (qb + 1) * qblk - 1) // CHUNK + 1) * CHUNK, nkeys)
    nkc = (n_adm_max + kc - 1) // kc
    qit = qi_ref[...].T
    wt = (w_ref[...] * (IDX_HEADS ** -0.5)).T
    qip = [jnp.concatenate([qit[h * IDX_DIM:(h + 1) * IDX_DIM] for h in (2 * j, 2 * j + 1)], axis=1).astype(BF16)
           for j in range(IDX_HEADS // 2)]
    wh = [wt[IDX_DIM + h:IDX_DIM + h + 1] for h in range(IDX_HEADS)]
    kpos = lax.broadcasted_iota(I32, (kc, qblk), 0)

    def score_chunk(c, carry):
        mn, mx = carry
        base = pl.multiple_of(c * kc, kc)
        kic = kib_s[pl.ds(base, kc), :]
        sc = jnp.zeros((kc, qblk), F32)
        for j in range(IDX_HEADS // 2):
            rel = jnp.dot(kic, qip[j], preferred_element_type=F32)
            sc = (sc + wh[2 * j] * jnp.maximum(rel[:, :qblk], 0.0)
                  + wh[2 * j + 1] * jnp.maximum(rel[:, qblk:], 0.0))
        adm = (base + kpos) < n_adm
        sc_s[c] = jnp.where(adm, sc, NEG_INF)
        mn = jnp.minimum(mn, jnp.min(jnp.where(adm, sc, jnp.inf), axis=0, keepdims=True))
        mx = jnp.maximum(mx, jnp.max(jnp.where(adm, sc, -jnp.inf), axis=0, keepdims=True))
        return mn, mx

    mn, mx = lax.fori_loop(0, nkc, score_chunk,
                           (jnp.full((1, qblk), jnp.inf, F32), jnp.full((1, qblk), -jnp.inf, F32)))

    def count(pred):
        def body(c, acc):
            hit = jnp.where(pred(c, sc_s[c]), 1.0, 0.0)
            for i in range(kc // acc_rows):
                acc = acc + hit[i * acc_rows:(i + 1) * acc_rows]
            return acc
        acc = lax.fori_loop(0, nkc, body, jnp.zeros((acc_rows, qblk), F32))
        return jnp.sum(acc, axis=0, keepdims=True)

    def bracket_extremes(lo, hi):
        def body(c, carry):
            vmin, vmax = carry
            s = sc_s[c]
            a = jnp.where(s >= lo, s, jnp.inf)
            b = jnp.where(s < hi, s, -jnp.inf)
            for i in range(kc // acc_rows):
                vmin = jnp.minimum(vmin, a[i * acc_rows:(i + 1) * acc_rows])
                vmax = jnp.maximum(vmax, b[i * acc_rows:(i + 1) * acc_rows])
            return vmin, vmax
        vmin, vmax = lax.fori_loop(0, nkc, body, (jnp.full((acc_rows, qblk), jnp.inf, F32),
                                                  jnp.full((acc_rows, qblk), -jnp.inf, F32)))
        return jnp.min(vmin, axis=0, keepdims=True), jnp.max(vmax, axis=0, keepdims=True)

    cl0 = jnp.where(n_adm <= topk, kf, n_adm.astype(F32))
    hi0 = mx + (jnp.abs(mx) + jnp.abs(mn) + 1e-30) * 1e-6

    def mid_of(lo, hi):
        return lo + (hi - lo) * 0.5

    def bis_cond(cy):
        pending, it = cy[4], cy[5]
        return jnp.logical_and(it < 600, jnp.max(pending) > 0.0)

    def bis_pass(lo, hi, cl, ch):
        mid = mid_of(lo, hi)
        cnt = count(lambda c, s: s >= mid)
        inside = (mid > lo) & (mid < hi)
        up = inside & (cnt >= kf)
        dn = inside & (cnt < kf)
        return jnp.where(up, mid, lo), jnp.where(dn, mid, hi), jnp.where(up, cnt, cl), jnp.where(dn, cnt, ch)

    def bis_body(cy):
        lo, hi, cl, ch, _, it = cy
        for _ in range(BIS_PASSES_PER_CHECK):
            lo, hi, cl, ch = bis_pass(lo, hi, cl, ch)
        lo, top = bracket_extremes(lo, hi)
        mid = mid_of(lo, hi)
        done = (cl == kf) | (lo == top) | jnp.logical_not((mid > lo) & (mid < hi))
        return lo, hi, cl, ch, jnp.where(done, 0.0, 1.0), it + 1

    lo, hi, cl, ch, _, _ = lax.while_loop(
        bis_cond, bis_body,
        (mn, hi0, cl0, jnp.zeros((1, qblk), F32), jnp.where(cl0 == kf, 0.0, 1.0), jnp.int32(0)))

    need = kf - ch

    def tie_limit():
        def bit(i, j):
            jt = j + lax.shift_left(jnp.int32(1), jnp.asarray(jbits - 1 - i, I32))
            cnt = count(lambda c, s: (s >= lo) & (s < hi) & ((c * kc + kpos) < jt))
            return jnp.where(cnt <= need, jt, j)
        return lax.fori_loop(0, jbits, bit, jnp.zeros((1, qblk), I32))

    has_tie = jnp.max(jnp.where(cl > kf, 1.0, 0.0)) > 0.0
    jl = lax.cond(has_tie, tie_limit, lambda: jnp.full((1, qblk), 2 ** 30, I32))

    def mask_chunk(c, carry):
        s = sc_s[c]
        sel = ((s >= hi) | ((s >= lo) & ((c * kc + kpos) < jl))) & (s > 0.5 * NEG_INF)
        sc_s[c] = jnp.where(sel, 0.0, NEG_INF)
        return carry

    lax.fori_loop(0, nkc, mask_chunk, 0)

    qh_s[...] = (q_ref[...] * (DSA_HEAD_DIM ** -0.5)).T.astype(BF16)

    def logit_chunk(c, ms):
        base = pl.multiple_of(c * kc, kc)
        bias = sc_s[c]
        kch = kb_s[pl.ds(base, kc), :]
        out = []
        for h in range(DSA_HEADS):
            n = h // DSA_GROUP
            lg = jnp.dot(kch[:, n * DSA_HEAD_DIM:(n + 1) * DSA_HEAD_DIM],
                         qh_s[h * DSA_HEAD_DIM:(h + 1) * DSA_HEAD_DIM, :], preferred_element_type=F32) + bias
            lg_s[c, h] = lg
            out.append(jnp.maximum(ms[h], jnp.max(lg, axis=0, keepdims=True)))
        return tuple(out)

    ms = lax.fori_loop(0, nkc, logit_chunk, tuple(jnp.full((1, qblk), NEG_INF, F32) for _ in range(DSA_HEADS)))

    def value_chunk(c, carry):
        ls, accs = carry
        ls_new, accs_new = [], []
        for h in range(DSA_HEADS):
            n = h // DSA_GROUP
            p = jnp.exp(lg_s[c, h] - ms[h])
            ls_new.append(ls[h] + jnp.sum(p, axis=0, keepdims=True))
            accs_new.append(accs[h] + jnp.dot(vt_s[c, n * DSA_HEAD_DIM:(n + 1) * DSA_HEAD_DIM, :], p.astype(BF16),
                                              preferred_element_type=F32))
        return tuple(ls_new), tuple(accs_new)

    init = (tuple(jnp.zeros((1, qblk), F32) for _ in range(DSA_HEADS)),
            tuple(jnp.zeros((DSA_HEAD_DIM, qblk), F32) for _ in range(DSA_HEADS)))
    ls, accs = lax.fori_loop(0, nkc, value_chunk, init)
    out = jnp.concatenate([accs[h] / ls[h] for h in range(DSA_HEADS)], axis=0)
    o_ref[...] = out.T


def _dsa(q, qi, kiw, new_keys, cache_keys, nb, t, topk):
    past = cache_keys[0].shape[1] if cache_keys else 0
    kc = DSA_KEY_CHUNK
    lp = ((past + t + kc - 1) // kc) * kc
    qblk = min(t, LANE)
    assert past % kc == 0 and kc >= topk and t % qblk == 0 and qblk % CHUNK == 0
    nq = t // qblk
    jbits = int(math.ceil(math.log2(lp))) + 1

    def qspec(width):
        return pl.BlockSpec((qblk, width), lambda b, i: (b * nq + i, 0))

    def kspec(a):
        return pl.BlockSpec((None,) + a.shape[1:], lambda b, i: (b, 0, 0))

    keys = (tuple(cache_keys) if cache_keys else ()) + tuple(new_keys)
    return pl.pallas_call(
        functools.partial(_dsa_kernel, past=past, nkeys=past + t, topk=topk, kc=kc, qblk=qblk, jbits=jbits),
        grid=(nb, nq),
        in_specs=[qspec(DSA_WIDTH), qspec(IDX_HEADS * IDX_DIM), qspec(LANE)] + [kspec(a) for a in keys],
        out_specs=qspec(DSA_WIDTH),
        out_shape=jax.ShapeDtypeStruct((nb * t, DSA_WIDTH), F32),
        scratch_shapes=[pltpu.VMEM((lp, LANE), BF16), pltpu.VMEM((lp // kc, LANE, kc), BF16),
                        pltpu.VMEM((lp, IDX_DIM), BF16), pltpu.VMEM((lp // kc, kc, qblk), F32),
                        pltpu.VMEM((DSA_WIDTH, qblk), BF16), pltpu.VMEM((lp // kc, DSA_HEADS, kc, qblk), F32)],
        compiler_params=_cparams(("parallel", "arbitrary")),
        name="dsa",
    )(q, qi, kiw, *keys)


def _mem_kernel(q_ref, mk_ref, mv_ref, o_ref):
    scale = MEM_HEAD_DIM ** -0.5
    for h in range(MEM_HEADS):
        sl = slice(h * MEM_HEAD_DIM, (h + 1) * MEM_HEAD_DIM)
        qh = q_ref[:, sl].astype(BF16)
        kh = mk_ref[:, sl].astype(BF16)
        vh = mv_ref[:, sl].astype(BF16)
        lg = lax.dot_general(qh, kh, NT_DIMS, preferred_element_type=F32) * scale
        p = jnp.exp(lg - jnp.max(lg, axis=-1, keepdims=True))
        o = jnp.dot(p.astype(BF16), vh, preferred_element_type=F32)
        o_ref[:, sl] = o / jnp.sum(p, axis=-1, keepdims=True)


def _mem_attn(h, mk, mv, kcol, vcol, row0, nb, t):
    tq = _tile(t, 256)
    nt = t // tq
    rb0 = row0 // tq
    m = mk.shape[1]
    return pl.pallas_call(
        _mem_kernel,
        grid=(nb, nt),
        in_specs=[pl.BlockSpec((tq, MEM_WIDTH), lambda b, i: (rb0 + b * nt + i, C_MQ // MEM_WIDTH)),
                  pl.BlockSpec((None, m, MEM_WIDTH), lambda b, i: (b, 0, kcol)),
                  pl.BlockSpec((None, m, MEM_WIDTH), lambda b, i: (b, 0, vcol))],
        out_specs=pl.BlockSpec((tq, MEM_WIDTH), lambda b, i: (b * nt + i, 0)),
        out_shape=jax.ShapeDtypeStruct((nb * t, MEM_WIDTH), F32),
        compiler_params=_cparams(("parallel", "parallel")),
        name="mem_attn",
    )(h, mk, mv)


def _merge_kernel(*refs, alpha, first_tiles):
    br_a, br_b = refs[0:N_BRANCH], refs[N_BRANCH:2 * N_BRANCH]
    g_refs = refs[2 * N_BRANCH:3 * N_BRANCH]
    x_ref, wb_ref, wo_ref, lg_ref, lb_ref, o_ref = refs[3 * N_BRANCH:]
    in_first = pl.program_id(0) < first_tiles
    merged = None
    for n in range(N_BRANCH):
        br = jnp.where(in_first, br_a[n][...], br_b[n][...])
        proj = jnp.dot(br.astype(BF16), wb_ref[n], preferred_element_type=F32)
        term = _sigmoid(g_refs[n][...]) * proj
        merged = term if merged is None else merged + term
    m = jnp.dot(merged.astype(BF16), wo_ref[...], preferred_element_type=F32)
    o_ref[...] = _layer_norm(alpha * x_ref[...] + m, lg_ref[...], lb_ref[...])


def _merge(branches_a, branches_b, h, x, wb, wo, lng, lnb, alpha):
    n = x.shape[0]
    rows_a, rows_b = branches_a[0].shape[0], branches_b[0].shape[0]
    tm = _tile(math.gcd(rows_a, rows_b), 256)
    ta, tb = rows_a // tm, rows_b // tm
    assert rows_a + rows_b == n
    aspec = pl.BlockSpec((tm, BRANCH_WIDTH), lambda i: (jnp.minimum(i, ta - 1), 0))
    bspec = pl.BlockSpec((tm, BRANCH_WIDTH), lambda i: (jnp.maximum(i - ta, 0), 0))

    def gspec(j):
        return pl.BlockSpec((tm, D_MODEL), lambda i: (i, C_GATE // D_MODEL + j))

    xspec = pl.BlockSpec((tm, D_MODEL), lambda i: (i, 0))
    vspec = pl.BlockSpec((1, D_MODEL), lambda i: (0, 0))
    return pl.pallas_call(
        functools.partial(_merge_kernel, alpha=alpha, first_tiles=ta),
        grid=(ta + tb,),
        in_specs=[aspec] * N_BRANCH + [bspec] * N_BRANCH + [gspec(0), gspec(1), gspec(2), xspec,
                  pl.BlockSpec((N_BRANCH, BRANCH_WIDTH, D_MODEL), lambda i: (0, 0, 0)),
                  pl.BlockSpec((D_MODEL, D_MODEL), lambda i: (0, 0)), vspec, vspec],
        out_specs=xspec,
        out_shape=jax.ShapeDtypeStruct((n, D_MODEL), F32),
        compiler_params=_cparams(("arbitrary",)),
        name="merge",
    )(*branches_a, *branches_b, h, h, h, x, wb, wo, lng, lnb)


def _swiglu_partial(xb, wg_ref, wu_ref, wd_ref):
    g = jnp.dot(xb, wg_ref[...], preferred_element_type=F32)
    u = jnp.dot(xb, wu_ref[...], preferred_element_type=F32)
    a = (g * _sigmoid(g)) * u
    return jnp.dot(a.astype(BF16), wd_ref[...], preferred_element_type=F32)


def _ffn_kernel(x_ref, wg_ref, wu_ref, wd_ref, lg_ref, lb_ref, o_ref, xb_s, acc_s, *, alpha):
    k = pl.program_id(1)

    @pl.when(k == 0)
    def _():
        xb_s[...] = x_ref[...].astype(BF16)
        acc_s[...] = jnp.zeros_like(acc_s)

    acc_s[...] += _swiglu_partial(xb_s[...], wg_ref, wu_ref, wd_ref)

    @pl.when(k == pl.num_programs(1) - 1)
    def _():
        o_ref[...] = _layer_norm(alpha * x_ref[...] + acc_s[...], lg_ref[...], lb_ref[...])


def _ffn_dense(x, w_gu, w_down, lng, lnb, alpha):
    n = x.shape[0]
    f = w_down.shape[0]
    tm = _tile(n, 512)
    fc = _tile(f, 1536, LANE)
    nk = f // fc
    xspec = pl.BlockSpec((tm, D_MODEL), lambda i, k: (i, 0))
    vspec = pl.BlockSpec((1, D_MODEL), lambda i, k: (0, 0))
    return pl.pallas_call(
        functools.partial(_ffn_kernel, alpha=alpha),
        grid=(n // tm, nk),
        in_specs=[xspec,
                  pl.BlockSpec((D_MODEL, fc), lambda i, k: (0, k)),
                  pl.BlockSpec((D_MODEL, fc), lambda i, k: (0, nk + k)),
                  pl.BlockSpec((fc, D_MODEL), lambda i, k: (k, 0)), vspec, vspec],
        out_specs=xspec,
        out_shape=jax.ShapeDtypeStruct((n, D_MODEL), F32),
        scratch_shapes=[pltpu.VMEM((tm, D_MODEL), BF16), pltpu.VMEM((tm, D_MODEL), F32)],
        compiler_params=_cparams(("parallel", "arbitrary")),
        name="ffn_dense",
    )(x, w_gu, w_gu, w_down, lng, lnb)


def _router_kernel(x_ref, w_ref, e_ref, g_ref):
    x = x_ref[...]
    w = w_ref[...]
    xh = x.astype(BF16)
    xl = (x - xh.astype(F32)).astype(BF16)
    wh = w.astype(BF16)
    wl = (w - wh.astype(F32)).astype(BF16)
    lg = (jnp.dot(xh, wh, preferred_element_type=F32) + jnp.dot(xh, wl, preferred_element_type=F32)
          + jnp.dot(xl, wh, preferred_element_type=F32))
    lane = lax.broadcasted_iota(I32, lg.shape, 1).astype(F32)
    lg = jnp.where(lane < N_EXPERTS, lg, -jnp.inf)
    m1 = jnp.max(lg, axis=-1, keepdims=True)
    i1 = jnp.min(jnp.where(lg == m1, lane, float(LANE)), axis=-1, keepdims=True)
    lg2 = jnp.where(lane == i1, -jnp.inf, lg)
    m2 = jnp.max(lg2, axis=-1, keepdims=True)
    i2 = jnp.min(jnp.where(lg2 == m2, lane, float(LANE)), axis=-1, keepdims=True)
    e = jnp.exp(m2 - m1)
    g1 = 1.0 / (1.0 + e)
    g2 = e / (1.0 + e)
    e_ref[...] = jnp.where(lane == 0.0, i1, jnp.where(lane == 1.0, i2, 0.0)).astype(I32)
    g_ref[...] = jnp.where(lane == 0.0, g1, jnp.where(lane == 1.0, g2, 0.0))


def _router(x, w_router):
    n = x.shape[0]
    tm = _tile(n, 512)
    wp = jnp.pad(w_router, ((0, 0), (0, LANE - N_EXPERTS)))
    ospec = pl.BlockSpec((tm, LANE), lambda i: (i, 0))
    return pl.pallas_call(
        _router_kernel,
        grid=(n // tm,),
        in_specs=[pl.BlockSpec((tm, D_MODEL), lambda i: (i, 0)), pl.BlockSpec((D_MODEL, LANE), lambda i: (0, 0))],
        out_specs=[ospec, ospec],
        out_shape=[jax.ShapeDtypeStruct((n, LANE), I32), jax.ShapeDtypeStruct((n, LANE), F32)],
        compiler_params=_cparams(("parallel",)),
        name="router",
    )(x, wp)


def _row_copy(src, dst, s, d, sem):
    return pltpu.make_async_copy(src.at[pl.ds(s, 1)], dst.at[pl.ds(d, 1)], sem)


def _dispatch_kernel(pos_ref, x_ref, xs_in, xs_hbm, sem, *, tb):
    del xs_in
    i = pl.program_id(0)

    def start(r, c):
        t = i * tb + r
        for j in range(TOP_K):
            _row_copy(x_ref, xs_hbm, r, pos_ref[TOP_K * t + j], sem).start()
        return c

    lax.fori_loop(0, tb, start, 0)
    for j in range(TOP_K):
        pltpu.make_async_copy(x_ref, xs_hbm.at[pl.ds(0, tb)], sem).wait()


def _dispatch(pos, x, n_rows):
    n = x.shape[0]
    tb = _tile(n, 256)
    xs0 = jnp.zeros((n_rows, D_MODEL), F32)
    return pl.pallas_call(
        functools.partial(_dispatch_kernel, tb=tb),
        grid_spec=pltpu.PrefetchScalarGridSpec(
            num_scalar_prefetch=1, grid=(n // tb,),
            in_specs=[pl.BlockSpec((tb, D_MODEL), lambda i, p: (i, 0)), pl.BlockSpec(memory_space=pl.ANY)],
            out_specs=pl.BlockSpec(memory_space=pl.ANY),
            scratch_shapes=[pltpu.SemaphoreType.DMA(())]),
        out_shape=jax.ShapeDtypeStruct((n_rows, D_MODEL), F32),
        input_output_aliases={2: 0},
        compiler_params=pltpu.CompilerParams(dimension_semantics=("arbitrary",), has_side_effects=True),
        name="moe_dispatch",
    )(pos, x, xs0)


def _expert_kernel(te_ref, nu_ref, x_ref, wg_ref, wu_ref, wd_ref, o_ref, xb_s, acc_s):
    del te_ref
    i = pl.program_id(0)
    k = pl.program_id(1)
    used = i < nu_ref[0]

    @pl.when(jnp.logical_and(used, k == 0))
    def _():
        xb_s[...] = x_ref[...].astype(BF16)
        acc_s[...] = jnp.zeros_like(acc_s)

    @pl.when(used)
    def _():
        acc_s[...] += _swiglu_partial(xb_s[...], wg_ref, wu_ref, wd_ref)

    @pl.when(k == pl.num_programs(1) - 1)
    def _():
        o_ref[...] = jnp.where(used, acc_s[...], 0.0)


def _experts(tile_expert, n_used, xs, w_gu, w_down, tm):
    n_rows = xs.shape[0]
    f = w_down.shape[1]
    fc = _tile(f, 1024, LANE)
    nk = f // fc
    xspec = pl.BlockSpec((tm, D_MODEL), lambda i, k, te, nu: (i, 0))
    return pl.pallas_call(
        _expert_kernel,
        grid_spec=pltpu.PrefetchScalarGridSpec(
            num_scalar_prefetch=2, grid=(n_rows // tm, nk),
            in_specs=[xspec,
                      pl.BlockSpec((None, D_MODEL, fc), lambda i, k, te, nu: (te[i], 0, k)),
                      pl.BlockSpec((None, D_MODEL, fc), lambda i, k, te, nu: (te[i], 0, nk + k)),
                      pl.BlockSpec((None, fc, D_MODEL), lambda i, k, te, nu: (te[i], k, 0))],
            out_specs=xspec,
            scratch_shapes=[pltpu.VMEM((tm, D_MODEL), BF16), pltpu.VMEM((tm, D_MODEL), F32)]),
        out_shape=jax.ShapeDtypeStruct((n_rows, D_MODEL), F32),
        compiler_params=_cparams(("arbitrary", "arbitrary")),
        name="moe_experts",
    )(tile_expert, n_used, xs, w_gu, w_gu, w_down)


def _combine_kernel(pos_ref, ys_hbm, g_ref, x_ref, lg_ref, lb_ref, o_ref, buf, sem, *, tb, alpha):
    i = pl.program_id(0)

    def start(r, c):
        t = i * tb + r
        for j in range(TOP_K):
            _row_copy(ys_hbm, buf.at[j], pos_ref[TOP_K * t + j], r, sem.at[j]).start()
        return c

    lax.fori_loop(0, tb, start, 0)
    for j in range(TOP_K):
        pltpu.make_async_copy(ys_hbm.at[pl.ds(0, tb)], buf.at[j], sem.at[j]).wait()
    g = g_ref[...]
    f = g[:, 0:1] * buf[0] + g[:, 1:2] * buf[1]
    o_ref[...] = _layer_norm(alpha * x_ref[...] + f, lg_ref[...], lb_ref[...])


def _combine(pos, ys, gates, x, lng, lnb, alpha):
    n = x.shape[0]
    tb = _tile(n, 256)
    xspec = pl.BlockSpec((tb, D_MODEL), lambda i, p: (i, 0))
    vspec = pl.BlockSpec((1, D_MODEL), lambda i, p: (0, 0))
    return pl.pallas_call(
        functools.partial(_combine_kernel, tb=tb, alpha=alpha),
        grid_spec=pltpu.PrefetchScalarGridSpec(
            num_scalar_prefetch=1, grid=(n // tb,),
            in_specs=[pl.BlockSpec(memory_space=pl.ANY), pl.BlockSpec((tb, LANE), lambda i, p: (i, 0)),
                      xspec, vspec, vspec],
            out_specs=xspec,
            scratch_shapes=[pltpu.VMEM((TOP_K, tb, D_MODEL), F32), pltpu.SemaphoreType.DMA((TOP_K,))]),
        out_shape=jax.ShapeDtypeStruct((n, D_MODEL), F32),
        compiler_params=_cparams(("arbitrary",)),
        name="moe_combine",
    )(pos, ys, gates, x, lng, lnb)


def _moe(x, w_router, w_gu, w_down, lng, lnb, alpha):
    n = x.shape[0]
    tm = _tile(n, 512)
    eidx, gates = _router(x, w_router)
    e2 = eidx[:, :TOP_K].reshape(-1)
    onehot = (e2[:, None] == jnp.arange(N_EXPERTS, dtype=I32)[None, :]).astype(I32)
    csum = jnp.cumsum(onehot, axis=0)
    rank = jnp.take_along_axis(csum, e2[:, None], axis=1)[:, 0] - 1
    counts = csum[-1]
    padded = ((counts + tm - 1) // tm) * tm
    ends = jnp.cumsum(padded)
    pos = ((ends - padded)[e2] + rank).astype(I32)
    n_rows = TOP_K * n + N_EXPERTS * tm
    n_tiles = n_rows // tm
    tile_start = jnp.arange(n_tiles, dtype=I32) * tm
    tile_expert = jnp.minimum(jnp.sum((tile_start[:, None] >= ends[None, :]).astype(I32), axis=1),
                              N_EXPERTS - 1).astype(I32)
    n_used = (ends[-1:] // tm).astype(I32)
    xs = _dispatch(pos, x, n_rows)
    ys = _experts(tile_expert, n_used, xs, w_gu, w_down, tm)
    return _combine(pos, ys, gates, x, lng, lnb, alpha)


def _pack_w_in(w):
    d = w.shape[0]
    parts, start = [], 0
    for n in IN_SIZES:
        parts.append(w[:, start:start + n])
        start += n
    hq, hf, hi, hg, dq, dk, dv, iq, ik, iw, mq, gates = parts
    ikw = jnp.concatenate([ik, iw, jnp.zeros((d, LANE - IDX_DIM - IDX_HEADS), w.dtype)], axis=1)
    tail = jnp.zeros((d, H_COLS - (C_IK + LANE)), w.dtype)
    return jnp.concatenate([hq, hf, hi, hg, gates, dq, mq, iq, dk, dv, ikw, tail], axis=1).astype(BF16)


def kernel(x_prompt, x_sample, cache_dsa_k, cache_dsa_v, cache_idx_k, state_hgrn, cache_mem_k, cache_mem_v,
           mem_prompt, w_in, hgrn_lb_logits, hgrn_norm_g, w_branch, w_out, w_mem_kv, ln_g, ln_b, ffn_w_gate_up,
           ffn_w_down, moe_router, moe_w_gate_up, moe_w_down):
    depth = w_in.shape[0]
    alpha = (2 * depth) ** 0.25
    bp, tp, d = x_prompt.shape
    bs, ts, _ = x_sample.shape
    past = cache_dsa_k.shape[2]
    m = mem_prompt.shape[1]
    np_rows, ns_rows = bp * tp, bs * ts

    sm = jax.nn.softmax(hgrn_lb_logits.astype(F32), axis=0)
    lbs = jnp.cumsum(sm, axis=0) - sm[0:1]

    x = jnp.concatenate([x_prompt.reshape(np_rows, d), x_sample.reshape(ns_rows, d)], axis=0)
    n = x.shape[0]
    tables_p = _rope_tables(jnp.arange(tp, dtype=I32))
    tables_s = _rope_tables(past + jnp.arange(ts, dtype=I32))
    mem_flat = mem_prompt.reshape(bp * m, d)
    zeros_state = jnp.zeros((bp, HG_HEADS, HG_DK, HG_DV), F32)
    outs = {k: [] for k in ("pk", "pv", "pki", "ph", "pmk", "pmv", "sk", "sv", "ski", "sh")}
    for l in range(depth):
        h = _matmul(x, _pack_w_in(w_in[l]), _tile(n, IN_TM), IN_TN, "in_proj")
        mem_kv = _matmul(mem_flat, w_mem_kv[l].astype(BF16), _tile(bp * m, 512), 512, "mem_kv").reshape(bp, m, 2 * MEM_WIDTH)
        lb = lbs[l].reshape(1, HG_WIDTH)
        ng = hgrn_norm_g[l].reshape(1, HG_WIDTH).astype(F32)

        qp, kp, vp, qip, kip = _rope(h, tables_p, 0, bp, tp)
        kip32 = kip[:, :IDX_DIM]
        ohg_p, sh_p = _hgrn(h, lb, ng, zeros_state, 0, bp, tp)
        odsa_p = _dsa(qp, qip, kip,
                      (kp.reshape(bp, tp, LANE), vp.reshape(bp, tp, LANE), kip32.reshape(bp, tp, IDX_DIM)), None,
                      bp, tp, min(TOPK_MAX, tp // 4))
        omem_p = _mem_attn(h, mem_kv, mem_kv, 0, 1, 0, bp, tp)

        qs, ks, vs, qis, kis = _rope(h, tables_s, np_rows, bs, ts)
        kis32 = kis[:, :IDX_DIM]
        ohg_s, sh_s = _hgrn(h, lb, ng, state_hgrn[l], np_rows, bs, ts)
        odsa_s = _dsa(qs, qis, kis,
                      (ks.reshape(bs, ts, LANE), vs.reshape(bs, ts, LANE), kis32.reshape(bs, ts, IDX_DIM)),
                      (cache_dsa_k[l].reshape(bs, past, LANE), cache_dsa_v[l].reshape(bs, past, LANE), cache_idx_k[l]),
                      bs, ts, min(TOPK_MAX, (past + ts) // 4))
        omem_s = _mem_attn(h, cache_mem_k[l].reshape(bs, m, MEM_WIDTH), cache_mem_v[l].reshape(bs, m, MEM_WIDTH),
                           0, 0, np_rows, bs, ts)

        x = _merge((ohg_p, odsa_p, omem_p), (ohg_s, odsa_s, omem_s), h, x, w_branch[l].astype(BF16),
                   w_out[l].astype(BF16), ln_g[l, 0].reshape(1, d), ln_b[l, 0].reshape(1, d), alpha)
        if l % 2 == 0:
            x = _ffn_dense(x, ffn_w_gate_up[l // 2].astype(BF16), ffn_w_down[l // 2].astype(BF16),
                           ln_g[l, 1].reshape(1, d), ln_b[l, 1].reshape(1, d), alpha)
        else:
            x = _moe(x, moe_router[l // 2], moe_w_gate_up[l // 2].astype(BF16), moe_w_down[l // 2].astype(BF16),
                     ln_g[l, 1].reshape(1, d), ln_b[l, 1].reshape(1, d), alpha)

        outs["pk"].append(kp.reshape(bp, tp, DSA_KV_HEADS, DSA_HEAD_DIM))
        outs["pv"].append(vp.reshape(bp, tp, DSA_KV_HEADS, DSA_HEAD_DIM))
        outs["pki"].append(kip32.reshape(bp, tp, IDX_DIM))
        outs["ph"].append(sh_p)
        outs["pmk"].append(mem_kv[..., :MEM_WIDTH].reshape(bp, m, MEM_HEADS, MEM_HEAD_DIM))
        outs["pmv"].append(mem_kv[..., MEM_WIDTH:].reshape(bp, m, MEM_HEADS, MEM_HEAD_DIM))
        outs["sk"].append(ks.reshape(bs, ts, DSA_KV_HEADS, DSA_HEAD_DIM))
        outs["sv"].append(vs.reshape(bs, ts, DSA_KV_HEADS, DSA_HEAD_DIM))
        outs["ski"].append(kis32.reshape(bs, ts, IDX_DIM))
        outs["sh"].append(sh_s)

    st = {k: jnp.stack(v) for k, v in outs.items()}
    y_prompt = x[:np_rows].reshape(bp, tp, d)
    y_sample = x[np_rows:].reshape(bs, ts, d)
    return (y_prompt, y_sample, st["pk"], st["pv"], st["pki"], st["ph"], st["pmk"], st["pmv"],
            st["sk"], st["sv"], st["ski"], st["sh"])
```
